```python
import math
import jax, jax.numpy as jnp
from jax import lax
import numpy as np

D_MODEL = 2048
BATCH = 2
SEQ = 4096
DEPTH = 1

EPS = 1e-6
D_MIX = 2 * D_MODEL
CONV_WIDTH = 4
LRU_WIDTH = D_MIX // 2
LRU_HEADS = 16
LRU_HEAD_DIM = LRU_WIDTH // LRU_HEADS
LRU_C = 8.0
SSD_WIDTH = D_MIX - LRU_WIDTH
SSD_HEAD_DIM = 64
SSD_HEADS = SSD_WIDTH // SSD_HEAD_DIM
SSD_GROUPS = 4
HEADS_PER_GROUP = SSD_HEADS // SSD_GROUPS
SSD_STATE = 128
SSD_CHUNK = 128
SSD_GN = SSD_GROUPS * SSD_STATE
XBC_WIDTH = SSD_WIDTH + 2 * SSD_GN
O_LRU_GATE = 0
O_LRU_X = O_LRU_GATE + LRU_WIDTH
O_SSD_Z = O_LRU_X + LRU_WIDTH
O_SSD_XBC = O_SSD_Z + SSD_WIDTH
O_SSD_DT = O_SSD_XBC + XBC_WIDTH
IN_WIDTH = O_SSD_DT + SSD_HEADS
PEER_HEADS = 8
PEER_KEYS = 128
PEER_EXPERTS = PEER_KEYS * PEER_KEYS
PEER_TOPK = 16
PEER_QDIM = 256
PEER_HALF = PEER_QDIM // 2
PEER_BLOCK = 128

kernel_name = "hybrid_rglru_ssd_peer_adaln"


def rms_norm(x, g):
    xf = x.astype(jnp.float32)
    y = xf * lax.rsqrt(jnp.mean(xf * xf, axis=-1, keepdims=True) + EPS)
    return (y * g.astype(jnp.float32)).astype(x.dtype)


def causal_depthwise_conv(x, w, b):
    y = lax.conv_general_dilated(
        x, w[:, None, :].astype(x.dtype), window_strides=(1,),
        padding=[(CONV_WIDTH - 1, 0)], dimension_numbers=("NWC", "WIO", "NWC"),
        feature_group_count=x.shape[-1])
    return y + b.astype(x.dtype)


def _linear_combine(left, right):
    a_l, b_l = left
    a_r, b_r = right
    return a_l * a_r, a_r * b_l + b_r


def rg_lru(x, w_a, b_a, w_i, b_i, lam):
    bsz, seq, _ = x.shape
    xf = x.astype(jnp.float32).reshape(bsz, seq, LRU_HEADS, LRU_HEAD_DIM)
    r = jax.nn.sigmoid(jnp.einsum("bshi,hij->bshj", xf, w_a.astype(jnp.float32)) + b_a.astype(jnp.float32))
    i = jax.nn.sigmoid(jnp.einsum("bshi,hij->bshj", xf, w_i.astype(jnp.float32)) + b_i.astype(jnp.float32))
    log_a = -LRU_C * jax.nn.softplus(-lam.astype(jnp.float32)).reshape(LRU_HEADS, LRU_HEAD_DIM) * r
    a = jnp.exp(log_a).reshape(bsz, seq, LRU_WIDTH)
    bt = (jnp.sqrt(-jnp.expm1(2.0 * log_a)) * (i * xf)).reshape(bsz, seq, LRU_WIDTH)
    _, h = lax.associative_scan(_linear_combine, (a, bt), axis=1)
    return h.astype(x.dtype)


def ssd_chunked(x, dt, a, bm, cm):
    bsz, seq = x.shape[:2]
    nc = seq // SSD_CHUNK
    shp = (bsz, nc, SSD_CHUNK, SSD_GROUPS, HEADS_PER_GROUP)
    xc = (x * dt[..., None]).reshape(*shp, SSD_HEAD_DIM)
    a_cs = jnp.cumsum((dt * a).reshape(shp), axis=2)
    bc = bm.reshape(bsz, nc, SSD_CHUNK, SSD_GROUPS, SSD_STATE)
    cc = cm.reshape(bsz, nc, SSD_CHUNK, SSD_GROUPS, SSD_STATE)
    a_t = jnp.moveaxis(a_cs, 2, -1)
    seg = a_t[..., :, None] - a_t[..., None, :]
    causal = jnp.tril(jnp.ones((SSD_CHUNK, SSD_CHUNK), dtype=bool))
    decay_in = jnp.exp(jnp.where(causal, seg, -jnp.inf))
    cb = jnp.einsum("bclgn,bcsgn->bcgls", cc, bc)
    y_diag = jnp.einsum("bcgrls,bcsgrp->bclgrp", cb[:, :, :, None] * decay_in, xc)
    decay_to_end = jnp.exp(a_cs[:, :, -1:] - a_cs)
    states = jnp.einsum("bclgn,bclgrp->bcgrpn", bc, xc * decay_to_end[..., None])
    chunk_decay = jnp.exp(a_cs[:, :, -1])

    def carry_state(h, inp):
        dec, st = inp
        return dec[..., None, None] * h + st, h

    h0 = jnp.zeros((bsz, SSD_GROUPS, HEADS_PER_GROUP, SSD_HEAD_DIM, SSD_STATE), jnp.float32)
    _, prev = lax.scan(carry_state, h0, (jnp.moveaxis(chunk_decay, 1, 0), jnp.moveaxis(states, 1, 0)))
    prev = jnp.moveaxis(prev, 0, 1)
    y_off = jnp.einsum("bclgn,bcgrpn->bclgrp", cc, prev) * jnp.exp(a_cs)[..., None]
    return (y_diag + y_off).reshape(bsz, seq, SSD_HEADS, SSD_HEAD_DIM)


def mamba2_heads(z, xbc, dt_raw, conv_w, conv_b, dt_bias, a_log, d_skip, norm_g):
    bsz, seq, _ = z.shape
    xbc = jax.nn.silu(causal_depthwise_conv(xbc, conv_w, conv_b)).astype(jnp.float32)
    xs = xbc[..., :SSD_WIDTH].reshape(bsz, seq, SSD_HEADS, SSD_HEAD_DIM)
    bm = xbc[..., SSD_WIDTH:SSD_WIDTH + SSD_GN].reshape(bsz, seq, SSD_GROUPS, SSD_STATE)
    cm = xbc[..., SSD_WIDTH + SSD_GN:].reshape(bsz, seq, SSD_GROUPS, SSD_STATE)
    dt = jax.nn.softplus(dt_raw.astype(jnp.float32) + dt_bias.astype(jnp.float32))
    a = -jnp.exp(a_log.astype(jnp.float32))
    y = ssd_chunked(xs, dt, a, bm, cm) + d_skip.astype(jnp.float32)[:, None] * xs
    gshape = (bsz, seq, SSD_GROUPS, SSD_WIDTH // SSD_GROUPS)
    y = y.reshape(gshape) * jax.nn.silu(z.astype(jnp.float32)).reshape(gshape)
    y = y * lax.rsqrt(jnp.mean(y * y, axis=-1, keepdims=True) + EPS)
    return (y.reshape(bsz, seq, SSD_WIDTH) * norm_g.astype(jnp.float32)).astype(z.dtype)


def peer(h, w_q, sub_keys, u_tab, v_tab):
    bsz, seq, d = h.shape
    q = jnp.einsum("bsd,de->bse", h, w_q).astype(jnp.float32).reshape(bsz, seq, PEER_HEADS, 2, PEER_HALF)
    sub = jnp.einsum("bshtd,htkd->bshtk", q, sub_keys.astype(jnp.float32))
    sub_val, sub_idx = lax.top_k(sub, PEER_TOPK)
    cand_val = (sub_val[..., 0, :, None] + sub_val[..., 1, None, :]).reshape(bsz, seq, PEER_HEADS, PEER_TOPK * PEER_TOPK)
    cand_idx = (sub_idx[..., 0, :, None] * PEER_KEYS + sub_idx[..., 1, None, :]).reshape(bsz, seq, PEER_HEADS, PEER_TOPK * PEER_TOPK)
    top_val, top_pos = lax.top_k(cand_val, PEER_TOPK)
    expert = jnp.take_along_axis(cand_idx, top_pos, axis=-1)
    gate = jax.nn.softmax(top_val, axis=-1)
    nblk = (bsz * seq) // PEER_BLOCK
    hb = h.reshape(nblk, PEER_BLOCK, d)
    eb = expert.reshape(nblk, PEER_BLOCK, PEER_HEADS, PEER_TOPK)
    gb = gate.reshape(nblk, PEER_BLOCK, PEER_HEADS, PEER_TOPK)

    def block(args):
        xt, e, g = args
        u = jnp.take(u_tab, e, axis=0)
        act = jax.nn.gelu(jnp.einsum("td,thkd->thk", xt, u).astype(jnp.float32))
        v = jnp.take(v_tab, e, axis=0)
        return jnp.einsum("thk,thkd->td", (g * act).astype(v.dtype), v)

    out = lax.map(block, (hb, eb, gb))
    return out.reshape(bsz, seq, d).astype(h.dtype)


def hybrid_layer(x, mod, norm1_g, w_in, lru_conv_w, lru_conv_b, lru_w_a, lru_b_a, lru_w_i, lru_b_i,
                 lru_lambda, ssd_conv_w, ssd_conv_b, ssd_dt_bias, ssd_a_log, ssd_d, ssd_norm_g, w_out,
                 norm2_g, peer_w_q, peer_sub_keys, peer_u, peer_v):
    shift1, scale1, gate1, shift2, scale2, gate2 = jnp.split(mod[:, None, :], 6, axis=-1)
    h = rms_norm(x, norm1_g) * (1.0 + scale1) + shift1
    proj = jnp.einsum("bsd,de->bse", h, w_in)
    lru_gate = proj[..., O_LRU_GATE:O_LRU_GATE + LRU_WIDTH]
    lru_x = causal_depthwise_conv(proj[..., O_LRU_X:O_LRU_X + LRU_WIDTH], lru_conv_w, lru_conv_b)
    lru_out = jax.nn.gelu(lru_gate) * rg_lru(lru_x, lru_w_a, lru_b_a, lru_w_i, lru_b_i, lru_lambda)
    ssd_out = mamba2_heads(proj[..., O_SSD_Z:O_SSD_Z + SSD_WIDTH],
                           proj[..., O_SSD_XBC:O_SSD_XBC + XBC_WIDTH],
                           proj[..., O_SSD_DT:O_SSD_DT + SSD_HEADS],
                           ssd_conv_w, ssd_conv_b, ssd_dt_bias, ssd_a_log, ssd_d, ssd_norm_g)
    mix = jnp.einsum("bse,ed->bsd", jnp.concatenate([lru_out, ssd_out], axis=-1), w_out)
    x = x + gate1 * mix
    h2 = rms_norm(x, norm2_g) * (1.0 + scale2) + shift2
    return x + gate2 * peer(h2, peer_w_q, peer_sub_keys, peer_u, peer_v)


def setup_inputs(seed: int = 0) -> dict:
    key = jax.random.key(seed)
    ks = jax.random.split(key, 32)
    f32 = jnp.float32

    def nrm(k, shape, scale):
        return jax.random.normal(k, shape, f32) * scale

    L = DEPTH
    a0 = jax.random.uniform(ks[12], (L, LRU_WIDTH), f32, minval=0.9, maxval=0.999)
    p = a0 ** (1.0 / LRU_C)
    dt0 = jnp.exp(jax.random.uniform(ks[15], (L, SSD_HEADS), f32, minval=math.log(1e-3), maxval=math.log(1e-1)))
    return {
        "x": nrm(ks[0], (BATCH, SEQ, D_MODEL), 1.0),
        "c": nrm(ks[1], (BATCH, D_MODEL), 1.0),
        "w_ada": nrm(ks[2], (L, D_MODEL, 6 * D_MODEL), 0.5 * D_MODEL ** -0.5),
        "b_ada": nrm(ks[3], (L, 6 * D_MODEL), 0.02),
        "norm1_g": 1.0 + nrm(ks[4], (L, D_MODEL), 0.02),
        "w_in": nrm(ks[5], (L, D_MODEL, IN_WIDTH), D_MODEL ** -0.5),
        "lru_conv_w": nrm(ks[6], (L, CONV_WIDTH, LRU_WIDTH), CONV_WIDTH ** -0.5),
        "lru_conv_b": nrm(ks[7], (L, LRU_WIDTH), 0.02),
        "lru_w_a": nrm(ks[8], (L, LRU_HEADS, LRU_HEAD_DIM, LRU_HEAD_DIM), LRU_HEAD_DIM ** -0.5),
        "lru_b_a": nrm(ks[9], (L, LRU_HEADS, LRU_HEAD_DIM), 0.02),
        "lru_w_i": nrm(ks[10], (L, LRU_HEADS, LRU_HEAD_DIM, LRU_HEAD_DIM), LRU_HEAD_DIM ** -0.5),
        "lru_b_i": nrm(ks[11], (L, LRU_HEADS, LRU_HEAD_DIM), 0.02),
        "lru_lambda": jnp.log(p) - jnp.log1p(-p),
        "ssd_conv_w": nrm(ks[13], (L, CONV_WIDTH, XBC_WIDTH), CONV_WIDTH ** -0.5),
        "ssd_conv_b": nrm(ks[14], (L, XBC_WIDTH), 0.02),
        "ssd_dt_bias": dt0 + jnp.log(-jnp.expm1(-dt0)),
        "ssd_a_log": jnp.log(jax.random.uniform(ks[16], (L, SSD_HEADS), f32, minval=1.0, maxval=16.0)),
        "ssd_d": 1.0 + nrm(ks[17], (L, SSD_HEADS), 0.02),
        "ssd_norm_g": 1.0 + nrm(ks[18], (L, SSD_WIDTH), 0.02),
        "w_out": nrm(ks[19], (L, D_MIX, D_MODEL), D_MIX ** -0.5),
        "norm2_g": 1.0 + nrm(ks[20], (L, D_MODEL), 0.02),
        "peer_w_q": nrm(ks[21], (L, D_MODEL, PEER_HEADS * PEER_QDIM), D_MODEL ** -0.5),
        "peer_sub_keys": nrm(ks[22], (L, PEER_HEADS, 2, PEER_KEYS, PEER_HALF), PEER_HALF ** -0.5),
        "peer_u": nrm(ks[23], (L, PEER_EXPERTS, D_MODEL), D_MODEL ** -0.5),
        "peer_v": nrm(ks[24], (L, PEER_EXPERTS, D_MODEL), PEER_HEADS ** -0.5),
        "final_norm_g": 1.0 + nrm(ks[25], (D_MODEL,), 0.02),
    }


def reference(x, c, w_ada, b_ada, norm1_g, w_in, lru_conv_w, lru_conv_b, lru_w_a, lru_b_a, lru_w_i,
              lru_b_i, lru_lambda, ssd_conv_w, ssd_conv_b, ssd_dt_bias, ssd_a_log, ssd_d, ssd_norm_g,
              w_out, norm2_g, peer_w_q, peer_sub_keys, peer_u, peer_v, final_norm_g):
    c_act = jax.nn.silu(c)
    for l in range(DEPTH):
        mod = jnp.einsum("bd,de->be", c_act, w_ada[l]) + b_ada[l]
        x = hybrid_layer(x, mod, norm1_g[l], w_in[l], lru_conv_w[l], lru_conv_b[l], lru_w_a[l], lru_b_a[l],
                         lru_w_i[l], lru_b_i[l], lru_lambda[l], ssd_conv_w[l], ssd_conv_b[l], ssd_dt_bias[l],
                         ssd_a_log[l], ssd_d[l], ssd_norm_g[l], w_out[l], norm2_g[l], peer_w_q[l],
                         peer_sub_keys[l], peer_u[l], peer_v[l])
    return rms_norm(x, final_norm_g)
```

```python
import functools
import math

import jax
import jax.numpy as jnp
from jax import lax
from jax.experimental import pallas as pl
from jax.experimental.pallas import tpu as pltpu

F32 = jnp.float32
BF16 = jnp.bfloat16

EPS = 1e-6
CONV_WIDTH = 4
LRU_C = 8.0
LRU_HEAD_DIM = 128
SSD_HEAD_DIM = 64
SSD_GROUPS = 4
SSD_STATE = 128
SSD_CHUNK = 128
PEER_HEADS = 8
PEER_KEYS = 128
PEER_TOPK = 16
LANES = 128
SUBLANES = 8
NEG_BIG = -1e30
VMEM_LIMIT = 56 * 1024 * 1024


def _cparams(sem):
    return pltpu.CompilerParams(dimension_semantics=sem, vmem_limit_bytes=VMEM_LIMIT)


def _split3(x):
    hi = x.astype(BF16)
    r1 = x - hi.astype(F32)
    mid = r1.astype(BF16)
    lo = (r1 - mid.astype(F32)).astype(BF16)
    return hi, mid, lo


def _dot_sel_right(x, sel):
    return sum(jnp.dot(p, sel, preferred_element_type=F32) for p in _split3(x))


def _dot_sel_left(sel, x):
    return sum(jnp.dot(sel, p, preferred_element_type=F32) for p in _split3(x))


def _softplus(x):
    return jnp.maximum(x, 0.0) + jnp.log1p(jnp.exp(-jnp.abs(x)))


def _silu(x):
    return x * jax.nn.sigmoid(x)


def _gelu(x):
    return 0.5 * x * (1.0 + jnp.tanh(math.sqrt(2.0 / math.pi) * (x + 0.044715 * (x * x * x))))


def _ada_kernel(c_ref, w_ref, b_ref, o_ref, act_ref):
    nb = c_ref.shape[0]
    tn = w_ref.shape[1]

    @pl.when(pl.program_id(0) == 0)
    def _():
        cv = c_ref[...]
        act_ref[...] = _silu(cv)

    for k in range(tn // LANES):
        sl = slice(k * LANES, (k + 1) * LANES)
        wk = w_ref[:, sl]
        for b in range(nb):
            o_ref[b:b + 1, sl] = jnp.sum(wk * act_ref[b], axis=0, keepdims=True) + b_ref[:, sl]


def _ada(c, w_ada, b_ada):
    nb, d = c.shape
    n = w_ada.shape[1]
    tn = 1024
    cb = jnp.broadcast_to(c[:, :, None], (nb, d, LANES))
    return pl.pallas_call(
        _ada_kernel,
        grid=(n // tn,),
        in_specs=[
            pl.BlockSpec((nb, d, LANES), lambda j: (0, 0, 0)),
            pl.BlockSpec((d, tn), lambda j: (0, j)),
            pl.BlockSpec((1, tn), lambda j: (0, j)),
        ],
        out_specs=pl.BlockSpec((nb, tn), lambda j: (0, j)),
        out_shape=jax.ShapeDtypeStruct((nb, n), F32),
        scratch_shapes=[pltpu.VMEM((nb, d, LANES), F32)],
        compiler_params=_cparams(("arbitrary",)),
        name="ada",
    )(cb, w_ada, b_ada.reshape(1, n))


def _norm1_kernel(x_ref, mod_ref, g_ref, wdt_ref, h_ref, dt_ref):
    x = x_ref[...]
    y = x * lax.rsqrt(jnp.mean(x * x, axis=-1, keepdims=True) + EPS) * g_ref[...]
    h = (y * (1.0 + mod_ref[1:2, :]) + mod_ref[0:1, :]).astype(BF16)
    h_ref[...] = h
    dt_ref[...] = jnp.dot(h, wdt_ref[...], preferred_element_type=F32)


def _norm1(x2d, mod3, g, wdt, seq):
    t, d = x2d.shape
    tm = min(512, seq)
    per_batch = seq // tm
    return pl.pallas_call(
        _norm1_kernel,
        grid=(t // tm,),
        in_specs=[
            pl.BlockSpec((tm, d), lambda i: (i, 0)),
            pl.BlockSpec((None, 6, d), lambda i: (i // per_batch, 0, 0)),
            pl.BlockSpec((1, d), lambda i: (0, 0)),
            pl.BlockSpec((d, LANES), lambda i: (0, 0)),
        ],
        out_specs=[
            pl.BlockSpec((tm, d), lambda i: (i, 0)),
            pl.BlockSpec((tm, LANES), lambda i: (i, 0)),
        ],
        out_shape=[jax.ShapeDtypeStruct((t, d), BF16), jax.ShapeDtypeStruct((t, LANES), F32)],
        compiler_params=_cparams(("parallel",)),
        name="norm1",
    )(x2d, mod3, g.reshape(1, d), wdt)


def _inproj_kernel(h_ref, w_ref, o_ref, wb_ref):
    @pl.when(pl.program_id(1) == 0)
    def _():
        wb_ref[...] = w_ref[...].astype(BF16)

    o_ref[...] = jnp.dot(h_ref[...], wb_ref[...], preferred_element_type=F32)


def _inproj(h, w_in, n_main):
    t, d = h.shape
    tn = 1152
    tm = min(1024, t)
    return pl.pallas_call(
        _inproj_kernel,
        grid=(n_main // tn, t // tm),
        in_specs=[
            pl.BlockSpec((tm, d), lambda j, i: (i, 0)),
            pl.BlockSpec((d, tn), lambda j, i: (0, j)),
        ],
        out_specs=pl.BlockSpec((tm, tn), lambda j, i: (i, j)),
        out_shape=jax.ShapeDtypeStruct((t, n_main), F32),
        scratch_shapes=[pltpu.VMEM((d, tn), BF16)],
        compiler_params=_cparams(("parallel", "arbitrary")),
        name="inproj",
    )(h, w_in)


def _lru_kernel(g_ref, x_ref, cw_ref, cb_ref, wab_ref, ba_ref, bi_ref, lam_ref, o_ref,
                xpad, a_scr, b_scr, hs, hcar):
    tt, c = x_ref.shape
    t = pl.program_id(2)

    @pl.when(t == 0)
    def _():
        xpad[0:SUBLANES, :] = jnp.zeros((SUBLANES, c), F32)
        hcar[...] = jnp.zeros_like(hcar)

    xpad[SUBLANES:SUBLANES + tt, :] = x_ref[...]
    y = cb_ref[...]
    for k in range(CONV_WIDTH):
        off = SUBLANES - (CONV_WIDTH - 1) + k
        y = y + cw_ref[k:k + 1, :] * xpad[off:off + tt, :]
    xpad[0:SUBLANES, :] = xpad[tt:tt + SUBLANES, :]

    row = lax.broadcasted_iota(jnp.int32, (tt, LRU_HEAD_DIM), 0) & (SUBLANES - 1)
    for j in range(c // LRU_HEAD_DIM):
        sl = slice(j * LRU_HEAD_DIM, (j + 1) * LRU_HEAD_DIM)
        yj = y[:, sl]
        gm = jnp.dot(yj.astype(BF16), wab_ref[j], preferred_element_type=F32)
        r = jax.nn.sigmoid(gm[:, :LRU_HEAD_DIM] + ba_ref[:, sl])
        i = jax.nn.sigmoid(gm[:, LRU_HEAD_DIM:] + bi_ref[:, sl])
        log_a = (-LRU_C * _softplus(-lam_ref[:, sl])) * r
        a = jnp.exp(log_a)
        th = jnp.tanh(log_a)
        b = jnp.sqrt((-2.0 * th) / (1.0 - th)) * (i * yj)
        for s in (1, 2, 4):
            a_sh = pltpu.roll(a, s, 0)
            b_sh = pltpu.roll(b, s, 0)
            m = row >= s
            b = jnp.where(m, a * b_sh + b, b)
            a = jnp.where(m, a * a_sh, a)
        a_scr[:, sl] = a
        b_scr[:, sl] = b

    carry = hcar[...]
    for n in range(tt // SUBLANES):
        rs = slice(n * SUBLANES, (n + 1) * SUBLANES)
        hb = b_scr[rs, :] + a_scr[rs, :] * carry
        hs[rs, :] = hb
        carry = hb[SUBLANES - 1:SUBLANES, :]
    hcar[...] = carry
    o_ref[...] = (_gelu(g_ref[...]) * hs[...]).astype(BF16)


def _lru(proj, cw, cb, wab, ba, bi, lam, batch, seq, width, x_col0):
    t = proj.shape[0]
    c = 512
    tt = min(512, seq)
    nt = seq // tt
    ncb = width // c
    hpb = c // LRU_HEAD_DIM
    xoff = x_col0 // c
    return pl.pallas_call(
        _lru_kernel,
        grid=(batch, ncb, nt),
        in_specs=[
            pl.BlockSpec((tt, c), lambda b, j, i: (b * nt + i, j)),
            pl.BlockSpec((tt, c), lambda b, j, i: (b * nt + i, xoff + j)),
            pl.BlockSpec((CONV_WIDTH, c), lambda b, j, i: (0, j)),
            pl.BlockSpec((1, c), lambda b, j, i: (0, j)),
            pl.BlockSpec((hpb, LRU_HEAD_DIM, 2 * LRU_HEAD_DIM), lambda b, j, i: (j, 0, 0)),
            pl.BlockSpec((1, c), lambda b, j, i: (0, j)),
            pl.BlockSpec((1, c), lambda b, j, i: (0, j)),
            pl.BlockSpec((1, c), lambda b, j, i: (0, j)),
        ],
        out_specs=pl.BlockSpec((tt, c), lambda b, j, i: (b * nt + i, j)),
        out_shape=jax.ShapeDtypeStruct((t, width), BF16),
        scratch_shapes=[
            pltpu.VMEM((tt + SUBLANES, c), F32),
            pltpu.VMEM((tt, c), F32),
            pltpu.VMEM((tt, c), F32),
            pltpu.VMEM((tt, c), F32),
            pltpu.VMEM((1, c), F32),
        ],
        compiler_params=_cparams(("parallel", "parallel", "arbitrary")),
        name="lru",
    )(proj, proj, cw, cb, wab, ba, bi, lam)


def _ssd_kernel(z_ref, xbc_ref, dt_ref, cw_ref, cb_ref, dtb_ref, alog_ref, dsk_ref, ng_ref, e_ref, tri_ref,
                o_ref, xpad, st):
    L = SSD_CHUNK
    N = SSD_STATE
    w = z_ref.shape[1]
    gw = w // SSD_GROUPS
    gn = SSD_GROUPS * N

    @pl.when(pl.program_id(1) == 0)
    def _():
        xpad[0:SUBLANES, :] = jnp.zeros((SUBLANES, xpad.shape[1]), F32)
        st[...] = jnp.zeros_like(st)

    xpad[SUBLANES:SUBLANES + L, :] = xbc_ref[...]
    y = cb_ref[...]
    for k in range(CONV_WIDTH):
        off = SUBLANES - (CONV_WIDTH - 1) + k
        y = y + cw_ref[k:k + 1, :] * xpad[off:off + L, :]
    xpad[0:SUBLANES, :] = xpad[L:L + SUBLANES, :]
    act = _silu(y)
    xs = act[:, :w]
    bm = act[:, w:w + gn]
    cm = act[:, w + gn:]

    dt = _softplus(dt_ref[...] + dtb_ref[...])
    da = dt * (-jnp.exp(alog_ref[...]))
    a_cs = _dot_sel_left(tri_ref[...], da)
    a_cs_t = a_cs.T
    ea = jnp.exp(a_cs)
    dte = jnp.exp(a_cs[L - 1:L, :] - a_cs)
    sel = e_ref[...]
    dt_f = _dot_sel_right(dt, sel)
    ea_f = _dot_sel_right(ea, sel)
    dte_f = _dot_sel_right(dte, sel)
    xc = xs * dt_f
    xcb = xc.astype(BF16)
    xd = (xc * dte_f).astype(BF16)

    causal = lax.broadcasted_iota(jnp.int32, (L, L), 0) >= lax.broadcasted_iota(jnp.int32, (L, L), 1)
    lane = lax.broadcasted_iota(jnp.int32, (L, LANES), 1)
    low_half = lane < SSD_HEAD_DIM
    heads_per_group = gw // SSD_HEAD_DIM
    for g in range(SSD_GROUPS):
        gs = slice(g * gw, (g + 1) * gw)
        cg = cm[:, g * N:(g + 1) * N].astype(BF16)
        bg = bm[:, g * N:(g + 1) * N]
        cbm = lax.dot_general(cg, bg.astype(BF16), (((1,), (1,)), ((), ())), preferred_element_type=F32)
        bg_t = bg.T.astype(BF16)
        s_prev = st[g]
        y_off = jnp.dot(cg, s_prev.astype(BF16), preferred_element_type=F32) * ea_f[:, gs]
        s_new = jnp.dot(bg_t, xd[:, gs], preferred_element_type=F32)
        st[g] = ea_f[L - 1:L, gs] * s_prev + s_new
        for p in range(heads_per_group // 2):
            h0 = g * heads_per_group + 2 * p
            cs = slice(h0 * SSD_HEAD_DIM, (h0 + 2) * SSD_HEAD_DIM)
            ms = []
            for hh in (h0, h0 + 1):
                seg = a_cs[:, hh:hh + 1] - a_cs_t[hh:hh + 1, :]
                ms.append(cbm * jnp.exp(jnp.where(causal, seg, NEG_BIG)))
            lhs = jnp.concatenate(ms, axis=1).astype(BF16)
            pair = xcb[:, cs]
            zero = jnp.zeros_like(pair)
            rhs = jnp.concatenate([jnp.where(low_half, pair, zero), jnp.where(low_half, zero, pair)], axis=0)
            y_diag = jnp.dot(lhs, rhs, preferred_element_type=F32)
            yp = y_diag + y_off[:, 2 * p * SSD_HEAD_DIM:(2 * p + 2) * SSD_HEAD_DIM] + dsk_ref[:, cs] * xs[:, cs]
            xpad[SUBLANES:SUBLANES + L, cs] = yp

    yz = xpad[SUBLANES:SUBLANES + L, 0:w] * _silu(z_ref[...])
    for g in range(SSD_GROUPS):
        gs = slice(g * gw, (g + 1) * gw)
        v = yz[:, gs]
        v = v * lax.rsqrt(jnp.mean(v * v, axis=-1, keepdims=True) + EPS)
        o_ref[:, gs] = (v * ng_ref[:, gs]).astype(BF16)


def _ssd(proj, dt_raw, cw, cb, dtb, alog, dsk, ng, sel, tri, batch, seq, width, z_col0, xbc_col0):
    t = proj.shape[0]
    xw = cw.shape[1]
    L = SSD_CHUNK
    nc = seq // L
    zoff = z_col0 // width
    xoff = xbc_col0 // xw
    return pl.pallas_call(
        _ssd_kernel,
        grid=(batch, nc),
        in_specs=[
            pl.BlockSpec((L, width), lambda b, i: (b * nc + i, zoff)),
            pl.BlockSpec((L, xw), lambda b, i: (b * nc + i, xoff)),
            pl.BlockSpec((L, LANES), lambda b, i: (b * nc + i, 0)),
            pl.BlockSpec((CONV_WIDTH, xw), lambda b, i: (0, 0)),
            pl.BlockSpec((1, xw), lambda b, i: (0, 0)),
            pl.BlockSpec((1, LANES), lambda b, i: (0, 0)),
            pl.BlockSpec((1, LANES), lambda b, i: (0, 0)),
            pl.BlockSpec((1, width), lambda b, i: (0, 0)),
            pl.BlockSpec((1, width), lambda b, i: (0, 0)),
            pl.BlockSpec((LANES, width), lambda b, i: (0, 0)),
            pl.BlockSpec((L, L), lambda b, i: (0, 0)),
        ],
        out_specs=pl.BlockSpec((L, width), lambda b, i: (b * nc + i, 0)),
        out_shape=jax.ShapeDtypeStruct((t, width), BF16),
        scratch_shapes=[
            pltpu.VMEM((L + SUBLANES, xw), F32),
            pltpu.VMEM((SSD_GROUPS, SSD_STATE, width // SSD_GROUPS), F32),
        ],
        compiler_params=_cparams(("parallel", "arbitrary")),
        name="ssd",
    )(proj, proj, dt_raw, cw, cb, dtb, alog, dsk, ng, sel, tri)


def _outproj_kernel(lru_ref, ssd_ref, wo_ref, x_ref, mod_ref, g_ref, wq_ref, x1_ref, h2t_ref, q_ref, acc):
    k = pl.program_id(1)
    nk = pl.num_programs(1)

    @pl.when(k == 0)
    def _():
        acc[...] = jnp.zeros_like(acc)

    @pl.when(k < nk // 2)
    def _():
        acc[...] += jnp.dot(lru_ref[...], wo_ref[...], preferred_element_type=F32)

    @pl.when(k >= nk // 2)
    def _():
        acc[...] += jnp.dot(ssd_ref[...], wo_ref[...], preferred_element_type=F32)

    @pl.when(k == nk - 1)
    def _():
        x1 = x_ref[...] + mod_ref[2:3, :] * acc[...]
        x1_ref[...] = x1
        y = x1 * lax.rsqrt(jnp.mean(x1 * x1, axis=-1, keepdims=True) + EPS) * g_ref[...]
        h2 = y * (1.0 + mod_ref[4:5, :]) + mod_ref[3:4, :]
        h2t_ref[...] = h2.T.astype(BF16)
        q_ref[...] = jnp.dot(h2.astype(BF16), wq_ref[...], preferred_element_type=F32)


def _outproj(lru, ssd, wo, x2d, mod3, g, wq, seq):
    t, d = x2d.shape
    half = lru.shape[1]
    tm = min(512, seq)
    tk = 512
    nkh = half // tk
    per_batch = seq // tm
    nq = wq.shape[1]
    return pl.pallas_call(
        _outproj_kernel,
        grid=(t // tm, 2 * nkh),
        in_specs=[
            pl.BlockSpec((tm, tk), lambda i, k: (i, jnp.minimum(k, nkh - 1))),
            pl.BlockSpec((tm, tk), lambda i, k: (i, jnp.maximum(k - nkh, 0))),
            pl.BlockSpec((tk, d), lambda i, k: (k, 0)),
            pl.BlockSpec((tm, d), lambda i, k: (i, 0)),
            pl.BlockSpec((None, 6, d), lambda i, k: (i // per_batch, 0, 0)),
            pl.BlockSpec((1, d), lambda i, k: (0, 0)),
            pl.BlockSpec((d, nq), lambda i, k: (0, 0)),
        ],
        out_specs=[
            pl.BlockSpec((tm, d), lambda i, k: (i, 0)),
            pl.BlockSpec((d, tm), lambda i, k: (0, i)),
            pl.BlockSpec((tm, nq), lambda i, k: (i, 0)),
        ],
        out_shape=[
            jax.ShapeDtypeStruct((t, d), F32),
            jax.ShapeDtypeStruct((d, t), BF16),
            jax.ShapeDtypeStruct((t, nq), F32),
        ],
        scratch_shapes=[pltpu.VMEM((tm, d), F32)],
        compiler_params=_cparams(("parallel", "arbitrary")),
        name="outproj",
    )(lru, ssd, wo, x2d, mod3, g.reshape(1, d), wq)


def _top_values(s, count):
    rid = lax.broadcasted_iota(jnp.int32, (count, s.shape[1]), 0)
    out = jnp.zeros((count, s.shape[1]), F32)
    for r in range(count):
        m = jnp.max(s, axis=0, keepdims=True)
        out = jnp.where(rid == r, m, out)
        s = jnp.where(s == m, NEG_BIG, s)
    return out


def _pair_candidates(v1, v2):
    parts = [v1[0:1] + v2]
    parts += [v1[a:a + 1] + v2[0:8] for a in (1, 2, 3)]
    parts += [v1[0:8] + v2[b:b + 1] for b in (0, 1, 2)]
    parts.append(v1[8:16] + v2[0:1])
    return jnp.concatenate(parts, axis=0)


def _kth_and_expsum(cand, count):
    z = jnp.zeros((1, cand.shape[1]), F32)
    m = None
    for _ in range(count):
        m = jnp.max(cand, axis=0, keepdims=True)
        z = z + jnp.exp2(m)
        cand = jnp.where(cand == m, NEG_BIG, cand)
    return m, z


def _route_kernel(q_ref, keys_ref, s1_ref, s2_ref, tau_ref):
    for h in range(PEER_HEADS):
        ss = []
        for half in range(2):
            col = (2 * h + half) * LANES
            qh = q_ref[:, col:col + LANES].astype(BF16)
            s = lax.dot_general(keys_ref[h, half], qh, (((1,), (1,)), ((), ())), preferred_element_type=F32)
            ss.append(s - jnp.max(s, axis=0, keepdims=True))
        s1, s2 = ss
        v1 = _top_values(s1, PEER_TOPK)
        v2 = _top_values(s2, PEER_TOPK)
        _, z = _kth_and_expsum(_pair_candidates(v1, v2), PEER_TOPK)
        lz = jnp.log2(z)
        s1 = s1 - lz
        tau, _ = _kth_and_expsum(_pair_candidates(v1 - lz, v2), PEER_TOPK)
        s1_ref[h] = s1
        s2_ref[h] = s2
        tau_ref[h:h + 1, :] = tau


def _route(q, keys):
    t, nq = q.shape
    tb = min(512, t)
    nk = keys.shape[2]
    return pl.pallas_call(
        _route_kernel,
        grid=(t // tb,),
        in_specs=[
            pl.BlockSpec((tb, nq), lambda i: (i, 0)),
            pl.BlockSpec(keys.shape, lambda i: (0, 0, 0, 0)),
        ],
        out_specs=[
            pl.BlockSpec((PEER_HEADS, nk, tb), lambda i: (0, 0, i)),
            pl.BlockSpec((PEER_HEADS, nk, tb), lambda i: (0, 0, i)),
            pl.BlockSpec((PEER_HEADS, tb), lambda i: (0, i)),
        ],
        out_shape=[
            jax.ShapeDtypeStruct((PEER_HEADS, nk, t), F32),
            jax.ShapeDtypeStruct((PEER_HEADS, nk, t), F32),
            jax.ShapeDtypeStruct((PEER_HEADS, t), F32),
        ],
        compiler_params=_cparams(("parallel",)),
        name="route",
    )(q, keys)


def _peer_kernel(h2t_ref, u_ref, vt_ref, s1_ref, s2_ref, tau_ref, o_ref, at_scr, gt_scr, *, tc):
    e = pl.program_id(1)
    ts = h2t_ref.shape[1]
    eb = u_ref.shape[0]

    @pl.when(e == 0)
    def _():
        o_ref[...] = jnp.zeros_like(o_ref)

    for c in range(ts // tc):
        cs = slice(c * tc, (c + 1) * tc)
        at_scr[...] = jnp.dot(u_ref[...], h2t_ref[:, cs], preferred_element_type=F32)

        def tile(it, carry):
            rows = pl.ds(pl.multiple_of(it * PEER_KEYS, PEER_KEYS), PEER_KEYS)
            wsum = jnp.zeros((PEER_KEYS, tc), F32)
            for h in range(PEER_HEADS):
                tsum = s1_ref[h, pl.ds(it, 1), cs] + s2_ref[h, :, cs]
                wsum = wsum + jnp.where(tsum >= tau_ref[h:h + 1, cs], jnp.exp2(tsum), 0.0)
            gt_scr[rows, :] = (_gelu(at_scr[rows, :]) * wsum).astype(BF16)
            return carry

        lax.fori_loop(0, eb // PEER_KEYS, tile, 0)
        o_ref[:, cs] += jnp.dot(vt_ref[...], gt_scr[...], preferred_element_type=F32)


def _peer(h2t, u, vt, s1, s2, tau):
    d, t = h2t.shape
    ne = u.shape[0]
    ts = min(1024, t)
    tc = min(512, ts)
    eb = 1024
    return pl.pallas_call(
        functools.partial(_peer_kernel, tc=tc),
        grid=(t // ts, ne // eb),
        in_specs=[
            pl.BlockSpec((d, ts), lambda s, e: (0, s)),
            pl.BlockSpec((eb, d), lambda s, e: (e, 0)),
            pl.BlockSpec((d, eb), lambda s, e: (0, e)),
            pl.BlockSpec((PEER_HEADS, eb // PEER_KEYS, ts), lambda s, e: (0, e, s)),
            pl.BlockSpec((PEER_HEADS, PEER_KEYS, ts), lambda s, e: (0, 0, s)),
            pl.BlockSpec((PEER_HEADS, ts), lambda s, e: (0, s)),
        ],
        out_specs=pl.BlockSpec((d, ts), lambda s, e: (0, s)),
        out_shape=jax.ShapeDtypeStruct((d, t), F32),
        scratch_shapes=[pltpu.VMEM((eb, tc), F32), pltpu.VMEM((eb, tc), BF16)],
        compiler_params=_cparams(("parallel", "arbitrary")),
        name="peer",
    )(h2t, u, vt, s1, s2, tau)


def _final_kernel(x1_ref, pt_ref, mod_ref, g_ref, o_ref):
    x2 = x1_ref[...] + mod_ref[5:6, :] * pt_ref[...].T
    o_ref[...] = x2 * lax.rsqrt(jnp.mean(x2 * x2, axis=-1, keepdims=True) + EPS) * g_ref[...]


def _final(x1, peer_t, mod3, g, seq):
    t, d = x1.shape
    tm = min(512, seq)
    per_batch = seq // tm
    return pl.pallas_call(
        _final_kernel,
        grid=(t // tm,),
        in_specs=[
            pl.BlockSpec((tm, d), lambda i: (i, 0)),
            pl.BlockSpec((d, tm), lambda i: (0, i)),
            pl.BlockSpec((None, 6, d), lambda i: (i // per_batch, 0, 0)),
            pl.BlockSpec((1, d), lambda i: (0, 0)),
        ],
        out_specs=pl.BlockSpec((tm, d), lambda i: (i, 0)),
        out_shape=jax.ShapeDtypeStruct((t, d), F32),
        compiler_params=_cparams(("parallel",)),
        name="final",
    )(x1, peer_t, mod3, g.reshape(1, d))


def _layer(x2d, mod3, batch, seq, norm1_g, w_in, lru_conv_w, lru_conv_b, lru_w_a, lru_b_a, lru_w_i, lru_b_i,
           lru_lambda, ssd_conv_w, ssd_conv_b, ssd_dt_bias, ssd_a_log, ssd_d, ssd_norm_g, w_out, norm2_g,
           peer_w_q, peer_sub_keys, peer_u, peer_v):
    d = x2d.shape[1]
    lru_w = lru_lambda.shape[0]
    ssd_w = ssd_norm_g.shape[0]
    xbc_w = ssd_conv_w.shape[1]
    n_heads = ssd_dt_bias.shape[0]
    o_lru_x = lru_w
    o_ssd_z = 2 * lru_w
    o_ssd_xbc = o_ssd_z + ssd_w
    o_ssd_dt = o_ssd_xbc + xbc_w

    wdt = jnp.pad(w_in[:, o_ssd_dt:o_ssd_dt + n_heads], ((0, 0), (0, LANES - n_heads))).astype(BF16)
    wab = jnp.concatenate([lru_w_a, lru_w_i], axis=-1).astype(BF16)
    pad_h = lambda v: jnp.pad(v, (0, LANES - n_heads)).reshape(1, LANES)
    head_of_channel = jnp.arange(ssd_w, dtype=jnp.int32) // SSD_HEAD_DIM
    sel = (jnp.arange(LANES, dtype=jnp.int32)[:, None] == head_of_channel[None, :]).astype(BF16)
    tri = (jnp.arange(SSD_CHUNK)[:, None] >= jnp.arange(SSD_CHUNK)[None, :]).astype(BF16)
    dsk = jnp.repeat(ssd_d, SSD_HEAD_DIM).reshape(1, ssd_w)

    h, dt_raw = _norm1(x2d, mod3, norm1_g, wdt, seq)
    proj = _inproj(h, w_in, o_ssd_dt)
    lru = _lru(proj, lru_conv_w, lru_conv_b.reshape(1, lru_w), wab, lru_b_a.reshape(1, lru_w),
               lru_b_i.reshape(1, lru_w), lru_lambda.reshape(1, lru_w), batch, seq, lru_w, o_lru_x)
    ssd = _ssd(proj, dt_raw, ssd_conv_w, ssd_conv_b.reshape(1, xbc_w), pad_h(ssd_dt_bias), pad_h(ssd_a_log),
               dsk, ssd_norm_g.reshape(1, ssd_w), sel, tri, batch, seq, ssd_w, o_ssd_z, o_ssd_xbc)
    x1, h2t, q = _outproj(lru, ssd, w_out.astype(BF16), x2d, mod3, norm2_g, peer_w_q.astype(BF16), seq)
    s1, s2, tau = _route(q, (peer_sub_keys * math.log2(math.e)).astype(BF16))
    peer_t = _peer(h2t, peer_u.astype(BF16), peer_v.T.astype(BF16), s1, s2, tau)
    return x1, peer_t


def kernel(x, c, w_ada, b_ada, norm1_g, w_in, lru_conv_w, lru_conv_b, lru_w_a, lru_b_a, lru_w_i, lru_b_i, lru_lambda, ssd_conv_w, ssd_conv_b, ssd_dt_bias, ssd_a_log, ssd_d, ssd_norm_g, w_out, norm2_g, peer_w_q, peer_sub_keys, peer_u, peer_v, final_norm_g):
    batch, seq, d = x.shape
    depth = w_ada.shape[0]
    assert depth == 1, "the final residual is fused with the last layer; only depth 1 is wired up"
    x2d = x.reshape(batch * seq, d)
    l = 0
    mod3 = _ada(c, w_ada[l], b_ada[l]).reshape(batch, 6, d)
    x1, peer_t = _layer(x2d, mod3, batch, seq, norm1_g[l], w_in[l], lru_conv_w[l], lru_conv_b[l], lru_w_a[l],
                        lru_b_a[l], lru_w_i[l], lru_b_i[l], lru_lambda[l], ssd_conv_w[l], ssd_conv_b[l],
                        ssd_dt_bias[l], ssd_a_log[l], ssd_d[l], ssd_norm_g[l], w_out[l], norm2_g[l],
                        peer_w_q[l], peer_sub_keys[l], peer_u[l], peer_v[l])
    y = _final(x1, peer_t, mod3, final_norm_g, seq)
    return y.reshape(batch, seq, d)
```

```python
import functools
import math

import jax
import jax.numpy as jnp
from jax import lax
from jax.experimental import pallas as pl
from jax.experimental.pallas import tpu as pltpu

F32 = jnp.float32
BF16 = jnp.bfloat16

EPS = 1e-6
CONV_WIDTH = 4
LRU_C = 8.0
LRU_HEAD_DIM = 128
SSD_HEAD_DIM = 64
SSD_GROUPS = 4
SSD_STATE = 128
SSD_CHUNK = 128
PEER_HEADS = 8
PEER_KEYS = 128
PEER_TOPK = 16
LANES = 128
SUBLANES = 8
NEG_BIG = -1e30
VMEM_LIMIT = 56 * 1024 * 1024


def _cparams(sem, flags=None):
    return pltpu.CompilerParams(dimension_semantics=sem, vmem_limit_bytes=VMEM_LIMIT, flags=flags)


def _split3(x):
    hi = x.astype(BF16)
    r1 = x - hi.astype(F32)
    mid = r1.astype(BF16)
    lo = (r1 - mid.astype(F32)).astype(BF16)
    return hi, mid, lo


def _dot_sel_right(x, sel):
    return sum(jnp.dot(p, sel, preferred_element_type=F32) for p in _split3(x))


def _dot_sel_left(sel, x):
    return sum(jnp.dot(sel, p, preferred_element_type=F32) for p in _split3(x))


def _softplus(x):
    return jnp.maximum(x, 0.0) + jnp.log1p(jnp.exp(-jnp.abs(x)))


def _silu(x):
    return x * jax.nn.sigmoid(x)


def _gelu(x):
    return 0.5 * x * (1.0 + jnp.tanh(math.sqrt(2.0 / math.pi) * (x + 0.044715 * (x * x * x))))


def _ada_kernel(c_ref, w_ref, b_ref, o_ref, act_ref):
    nb = c_ref.shape[0]
    tn = w_ref.shape[1]

    @pl.when(pl.program_id(0) == 0)
    def _():
        cv = c_ref[...]
        act_ref[...] = _silu(cv)

    for k in range(tn // LANES):
        sl = slice(k * LANES, (k + 1) * LANES)
        wk = w_ref[:, sl]
        for b in range(nb):
            o_ref[b:b + 1, sl] = jnp.sum(wk * act_ref[b], axis=0, keepdims=True) + b_ref[:, sl]


def _ada(c, w_ada, b_ada):
    nb, d = c.shape
    n = w_ada.shape[1]
    tn = 1024
    cb = jnp.broadcast_to(c[:, :, None], (nb, d, LANES))
    return pl.pallas_call(
        _ada_kernel,
        grid=(n // tn,),
        in_specs=[
            pl.BlockSpec((nb, d, LANES), lambda j: (0, 0, 0)),
            pl.BlockSpec((d, tn), lambda j: (0, j)),
            pl.BlockSpec((1, tn), lambda j: (0, j)),
        ],
        out_specs=pl.BlockSpec((nb, tn), lambda j: (0, j)),
        out_shape=jax.ShapeDtypeStruct((nb, n), F32),
        scratch_shapes=[pltpu.VMEM((nb, d, LANES), F32)],
        compiler_params=_cparams(("arbitrary",)),
        name="ada",
    )(cb, w_ada, b_ada.reshape(1, n))


def _norm1_kernel(x_ref, mod_ref, g_ref, wdt_ref, h_ref, dt_ref):
    x = x_ref[...]
    y = x * lax.rsqrt(jnp.mean(x * x, axis=-1, keepdims=True) + EPS) * g_ref[...]
    h = (y * (1.0 + mod_ref[1:2, :]) + mod_ref[0:1, :]).astype(BF16)
    h_ref[...] = h
    dt_ref[...] = jnp.dot(h, wdt_ref[...], preferred_element_type=F32)


def _norm1(x2d, mod3, g, wdt, seq):
    t, d = x2d.shape
    tm = min(512, seq)
    per_batch = seq // tm
    return pl.pallas_call(
        _norm1_kernel,
        grid=(t // tm,),
        in_specs=[
            pl.BlockSpec((tm, d), lambda i: (i, 0)),
            pl.BlockSpec((None, 6, d), lambda i: (i // per_batch, 0, 0)),
            pl.BlockSpec((1, d), lambda i: (0, 0)),
            pl.BlockSpec((d, LANES), lambda i: (0, 0)),
        ],
        out_specs=[
            pl.BlockSpec((tm, d), lambda i: (i, 0)),
            pl.BlockSpec((tm, LANES), lambda i: (i, 0)),
        ],
        out_shape=[jax.ShapeDtypeStruct((t, d), BF16), jax.ShapeDtypeStruct((t, LANES), F32)],
        compiler_params=_cparams(("parallel",)),
        name="norm1",
    )(x2d, mod3, g.reshape(1, d), wdt)


def _inproj_kernel(h_ref, w_ref, o_ref, wb_ref):
    @pl.when(pl.program_id(1) == 0)
    def _():
        wb_ref[...] = w_ref[...].astype(BF16)

    o_ref[...] = jnp.dot(h_ref[...], wb_ref[...], preferred_element_type=F32)


def _inproj(h, w_in, n_main):
    t, d = h.shape
    tn = 1152
    tm = min(1024, t)
    return pl.pallas_call(
        _inproj_kernel,
        grid=(n_main // tn, t // tm),
        in_specs=[
            pl.BlockSpec((tm, d), lambda j, i: (i, 0)),
            pl.BlockSpec((d, tn), lambda j, i: (0, j)),
        ],
        out_specs=pl.BlockSpec((tm, tn), lambda j, i: (i, j)),
        out_shape=jax.ShapeDtypeStruct((t, n_main), F32),
        scratch_shapes=[pltpu.VMEM((d, tn), BF16)],
        compiler_params=_cparams(("parallel", "arbitrary")),
        name="inproj",
    )(h, w_in)


def _lru_kernel(g_ref, x_ref, cw_ref, cb_ref, wab_ref, ba_ref, bi_ref, lam_ref, o_ref,
                xpad, a_scr, b_scr, hs, hcar):
    tt, c = x_ref.shape
    t = pl.program_id(2)

    @pl.when(t == 0)
    def _():
        xpad[0:SUBLANES, :] = jnp.zeros((SUBLANES, c), F32)
        hcar[...] = jnp.zeros_like(hcar)

    xpad[SUBLANES:SUBLANES + tt, :] = x_ref[...]
    y = cb_ref[...]
    for k in range(CONV_WIDTH):
        off = SUBLANES - (CONV_WIDTH - 1) + k
        y = y + cw_ref[k:k + 1, :] * xpad[off:off + tt, :]
    xpad[0:SUBLANES, :] = xpad[tt:tt + SUBLANES, :]

    row = lax.broadcasted_iota(jnp.int32, (tt, LRU_HEAD_DIM), 0) & (SUBLANES - 1)
    for j in range(c // LRU_HEAD_DIM):
        sl = slice(j * LRU_HEAD_DIM, (j + 1) * LRU_HEAD_DIM)
        yj = y[:, sl]
        gm = jnp.dot(yj.astype(BF16), wab_ref[j], preferred_element_type=F32)
        r = jax.nn.sigmoid(gm[:, :LRU_HEAD_DIM] + ba_ref[:, sl])
        i = jax.nn.sigmoid(gm[:, LRU_HEAD_DIM:] + bi_ref[:, sl])
        log_a = (-LRU_C * _softplus(-lam_ref[:, sl])) * r
        a = jnp.exp(log_a)
        th = jnp.tanh(log_a)
        b = jnp.sqrt((-2.0 * th) / (1.0 - th)) * (i * yj)
        for s in (1, 2, 4):
            a_sh = pltpu.roll(a, s, 0)
            b_sh = pltpu.roll(b, s, 0)
            m = row >= s
            b = jnp.where(m, a * b_sh + b, b)
            a = jnp.where(m, a * a_sh, a)
        a_scr[:, sl] = a
        b_scr[:, sl] = b

    carry = hcar[...]
    for n in range(tt // SUBLANES):
        rs = slice(n * SUBLANES, (n + 1) * SUBLANES)
        hb = b_scr[rs, :] + a_scr[rs, :] * carry
        hs[rs, :] = hb
        carry = hb[SUBLANES - 1:SUBLANES, :]
    hcar[...] = carry
    o_ref[...] = (_gelu(g_ref[...]) * hs[...]).astype(BF16)


def _lru(proj, cw, cb, wab, ba, bi, lam, batch, seq, width, x_col0):
    t = proj.shape[0]
    c = 512
    tt = min(512, seq)
    nt = seq // tt
    ncb = width // c
    hpb = c // LRU_HEAD_DIM
    xoff = x_col0 // c
    return pl.pallas_call(
        _lru_kernel,
        grid=(batch, ncb, nt),
        in_specs=[
            pl.BlockSpec((tt, c), lambda b, j, i: (b * nt + i, j)),
            pl.BlockSpec((tt, c), lambda b, j, i: (b * nt + i, xoff + j)),
            pl.BlockSpec((CONV_WIDTH, c), lambda b, j, i: (0, j)),
            pl.BlockSpec((1, c), lambda b, j, i: (0, j)),
            pl.BlockSpec((hpb, LRU_HEAD_DIM, 2 * LRU_HEAD_DIM), lambda b, j, i: (j, 0, 0)),
            pl.BlockSpec((1, c), lambda b, j, i: (0, j)),
            pl.BlockSpec((1, c), lambda b, j, i: (0, j)),
            pl.BlockSpec((1, c), lambda b, j, i: (0, j)),
        ],
        out_specs=pl.BlockSpec((tt, c), lambda b, j, i: (b * nt + i, j)),
        out_shape=jax.ShapeDtypeStruct((t, width), BF16),
        scratch_shapes=[
            pltpu.VMEM((tt + SUBLANES, c), F32),
            pltpu.VMEM((tt, c), F32),
            pltpu.VMEM((tt, c), F32),
            pltpu.VMEM((tt, c), F32),
            pltpu.VMEM((1, c), F32),
        ],
        compiler_params=_cparams(("parallel", "parallel", "arbitrary")),
        name="lru",
    )(proj, proj, cw, cb, wab, ba, bi, lam)


def _ssd_kernel(z_ref, xbc_ref, dt_ref, cw_ref, cb_ref, dtb_ref, alog_ref, dsk_ref, ng_ref, e_ref, tri_ref,
                o_ref, xpad, st):
    L = SSD_CHUNK
    N = SSD_STATE
    w = z_ref.shape[1]
    gw = w // SSD_GROUPS
    gn = SSD_GROUPS * N

    @pl.when(pl.program_id(1) == 0)
    def _():
        xpad[0:SUBLANES, :] = jnp.zeros((SUBLANES, xpad.shape[1]), F32)
        st[...] = jnp.zeros_like(st)

    xpad[SUBLANES:SUBLANES + L, :] = xbc_ref[...]
    y = cb_ref[...]
    for k in range(CONV_WIDTH):
        off = SUBLANES - (CONV_WIDTH - 1) + k
        y = y + cw_ref[k:k + 1, :] * xpad[off:off + L, :]
    xpad[0:SUBLANES, :] = xpad[L:L + SUBLANES, :]
    act = _silu(y)
    xs = act[:, :w]
    bm = act[:, w:w + gn]
    cm = act[:, w + gn:]

    dt = _softplus(dt_ref[...] + dtb_ref[...])
    da = dt * (-jnp.exp(alog_ref[...]))
    a_cs = _dot_sel_left(tri_ref[...], da)
    a_cs_t = a_cs.T
    ea = jnp.exp(a_cs)
    dte = jnp.exp(a_cs[L - 1:L, :] - a_cs)
    sel = e_ref[...]
    dt_f = _dot_sel_right(dt, sel)
    ea_f = _dot_sel_right(ea, sel)
    dte_f = _dot_sel_right(dte, sel)
    xc = xs * dt_f
    xcb = xc.astype(BF16)
    xd = (xc * dte_f).astype(BF16)

    causal = lax.broadcasted_iota(jnp.int32, (L, L), 0) >= lax.broadcasted_iota(jnp.int32, (L, L), 1)
    lane = lax.broadcasted_iota(jnp.int32, (L, LANES), 1)
    low_half = lane < SSD_HEAD_DIM
    heads_per_group = gw // SSD_HEAD_DIM
    for g in range(SSD_GROUPS):
        gs = slice(g * gw, (g + 1) * gw)
        cg = cm[:, g * N:(g + 1) * N].astype(BF16)
        bg = bm[:, g * N:(g + 1) * N]
        cbm = lax.dot_general(cg, bg.astype(BF16), (((1,), (1,)), ((), ())), preferred_element_type=F32)
        bg_t = bg.T.astype(BF16)
        s_prev = st[g]
        y_off = jnp.dot(cg, s_prev.astype(BF16), preferred_element_type=F32) * ea_f[:, gs]
        s_new = jnp.dot(bg_t, xd[:, gs], preferred_element_type=F32)
        st[g] = ea_f[L - 1:L, gs] * s_prev + s_new
        for p in range(heads_per_group // 2):
            h0 = g * heads_per_group + 2 * p
            cs = slice(h0 * SSD_HEAD_DIM, (h0 + 2) * SSD_HEAD_DIM)
            ms = []
            for hh in (h0, h0 + 1):
                seg = a_cs[:, hh:hh + 1] - a_cs_t[hh:hh + 1, :]
                ms.append(cbm * jnp.exp(jnp.where(causal, seg, NEG_BIG)))
            lhs = jnp.concatenate(ms, axis=1).astype(BF16)
            pair = xcb[:, cs]
            zero = jnp.zeros_like(pair)
            rhs = jnp.concatenate([jnp.where(low_half, pair, zero), jnp.where(low_half, zero, pair)], axis=0)
            y_diag = jnp.dot(lhs, rhs, preferred_element_type=F32)
            yp = y_diag + y_off[:, 2 * p * SSD_HEAD_DIM:(2 * p + 2) * SSD_HEAD_DIM] + dsk_ref[:, cs] * xs[:, cs]
            xpad[SUBLANES:SUBLANES + L, cs] = yp

    yz = xpad[SUBLANES:SUBLANES + L, 0:w] * _silu(z_ref[...])
    for g in range(SSD_GROUPS):
        gs = slice(g * gw, (g + 1) * gw)
        v = yz[:, gs]
        v = v * lax.rsqrt(jnp.mean(v * v, axis=-1, keepdims=True) + EPS)
        o_ref[:, gs] = (v * ng_ref[:, gs]).astype(BF16)


def _ssd(proj, dt_raw, cw, cb, dtb, alog, dsk, ng, sel, tri, batch, seq, width, z_col0, xbc_col0):
    t = proj.shape[0]
    xw = cw.shape[1]
    L = SSD_CHUNK
    nc = seq // L
    zoff = z_col0 // width
    xoff = xbc_col0 // xw
    return pl.pallas_call(
        _ssd_kernel,
        grid=(batch, nc),
        in_specs=[
            pl.BlockSpec((L, width), lambda b, i: (b * nc + i, zoff)),
            pl.BlockSpec((L, xw), lambda b, i: (b * nc + i, xoff)),
            pl.BlockSpec((L, LANES), lambda b, i: (b * nc + i, 0)),
            pl.BlockSpec((CONV_WIDTH, xw), lambda b, i: (0, 0)),
            pl.BlockSpec((1, xw), lambda b, i: (0, 0)),
            pl.BlockSpec((1, LANES), lambda b, i: (0, 0)),
            pl.BlockSpec((1, LANES), lambda b, i: (0, 0)),
            pl.BlockSpec((1, width), lambda b, i: (0, 0)),
            pl.BlockSpec((1, width), lambda b, i: (0, 0)),
            pl.BlockSpec((LANES, width), lambda b, i: (0, 0)),
            pl.BlockSpec((L, L), lambda b, i: (0, 0)),
        ],
        out_specs=pl.BlockSpec((L, width), lambda b, i: (b * nc + i, 0)),
        out_shape=jax.ShapeDtypeStruct((t, width), BF16),
        scratch_shapes=[
            pltpu.VMEM((L + SUBLANES, xw), F32),
            pltpu.VMEM((SSD_GROUPS, SSD_STATE, width // SSD_GROUPS), F32),
        ],
        compiler_params=_cparams(("parallel", "arbitrary")),
        name="ssd",
    )(proj, proj, dt_raw, cw, cb, dtb, alog, dsk, ng, sel, tri)


def _outproj_kernel(lru_ref, ssd_ref, wo_ref, x_ref, mod_ref, g_ref, wq_ref, x1_ref, h2t_ref, q_ref, acc):
    k = pl.program_id(1)
    nk = pl.num_programs(1)

    @pl.when(k == 0)
    def _():
        acc[...] = jnp.zeros_like(acc)

    @pl.when(k < nk // 2)
    def _():
        acc[...] += jnp.dot(lru_ref[...], wo_ref[...], preferred_element_type=F32)

    @pl.when(k >= nk // 2)
    def _():
        acc[...] += jnp.dot(ssd_ref[...], wo_ref[...], preferred_element_type=F32)

    @pl.when(k == nk - 1)
    def _():
        x1 = x_ref[...] + mod_ref[2:3, :] * acc[...]
        x1_ref[...] = x1
        y = x1 * lax.rsqrt(jnp.mean(x1 * x1, axis=-1, keepdims=True) + EPS) * g_ref[...]
        h2 = y * (1.0 + mod_ref[4:5, :]) + mod_ref[3:4, :]
        h2t_ref[...] = h2.T.astype(BF16)
        q_ref[...] = jnp.dot(h2.astype(BF16), wq_ref[...], preferred_element_type=F32)


def _outproj(lru, ssd, wo, x2d, mod3, g, wq, seq):
    t, d = x2d.shape
    half = lru.shape[1]
    tm = min(512, seq)
    tk = 512
    nkh = half // tk
    per_batch = seq // tm
    nq = wq.shape[1]
    return pl.pallas_call(
        _outproj_kernel,
        grid=(t // tm, 2 * nkh),
        in_specs=[
            pl.BlockSpec((tm, tk), lambda i, k: (i, jnp.minimum(k, nkh - 1))),
            pl.BlockSpec((tm, tk), lambda i, k: (i, jnp.maximum(k - nkh, 0))),
            pl.BlockSpec((tk, d), lambda i, k: (k, 0)),
            pl.BlockSpec((tm, d), lambda i, k: (i, 0)),
            pl.BlockSpec((None, 6, d), lambda i, k: (i // per_batch, 0, 0)),
            pl.BlockSpec((1, d), lambda i, k: (0, 0)),
            pl.BlockSpec((d, nq), lambda i, k: (0, 0)),
        ],
        out_specs=[
            pl.BlockSpec((tm, d), lambda i, k: (i, 0)),
            pl.BlockSpec((d, tm), lambda i, k: (0, i)),
            pl.BlockSpec((tm, nq), lambda i, k: (i, 0)),
        ],
        out_shape=[
            jax.ShapeDtypeStruct((t, d), F32),
            jax.ShapeDtypeStruct((d, t), BF16),
            jax.ShapeDtypeStruct((t, nq), F32),
        ],
        scratch_shapes=[pltpu.VMEM((tm, d), F32)],
        compiler_params=_cparams(("parallel", "arbitrary")),
        name="outproj",
    )(lru, ssd, wo, x2d, mod3, g.reshape(1, d), wq)


def _top_values(s, count):
    rid = lax.broadcasted_iota(jnp.int32, (count, s.shape[1]), 0)
    out = jnp.zeros((count, s.shape[1]), F32)
    for r in range(count):
        m = jnp.max(s, axis=0, keepdims=True)
        out = jnp.where(rid == r, m, out)
        s = jnp.where(s == m, NEG_BIG, s)
    return out


def _pair_candidates(v1, v2):
    parts = [v1[0:1] + v2]
    parts += [v1[a:a + 1] + v2[0:8] for a in (1, 2, 3)]
    parts += [v1[0:8] + v2[b:b + 1] for b in (0, 1, 2)]
    parts.append(v1[8:16] + v2[0:1])
    return jnp.concatenate(parts, axis=0)


def _kth_and_expsum(cand, count):
    z = jnp.zeros((1, cand.shape[1]), F32)
    m = None
    for _ in range(count):
        m = jnp.max(cand, axis=0, keepdims=True)
        z = z + jnp.exp2(m)
        cand = jnp.where(cand == m, NEG_BIG, cand)
    return m, z


def _route_kernel(q_ref, keys_ref, s1_ref, s2_ref, tau_ref):
    for h in range(PEER_HEADS):
        ss = []
        for half in range(2):
            col = (2 * h + half) * LANES
            qh = q_ref[:, col:col + LANES].astype(BF16)
            s = lax.dot_general(keys_ref[h, half], qh, (((1,), (1,)), ((), ())), preferred_element_type=F32)
            ss.append(s - jnp.max(s, axis=0, keepdims=True))
        s1, s2 = ss
        v1 = _top_values(s1, PEER_TOPK)
        v2 = _top_values(s2, PEER_TOPK)
        _, z = _kth_and_expsum(_pair_candidates(v1, v2), PEER_TOPK)
        lz = jnp.log2(z)
        s1 = s1 - lz
        tau, _ = _kth_and_expsum(_pair_candidates(v1 - lz, v2), PEER_TOPK)
        s1_ref[h] = s1
        s2_ref[h] = s2
        tau_ref[h:h + 1, :] = tau


def _route(q, keys):
    t, nq = q.shape
    tb = min(512, t)
    nk = keys.shape[2]
    return pl.pallas_call(
        _route_kernel,
        grid=(t // tb,),
        in_specs=[
            pl.BlockSpec((tb, nq), lambda i: (i, 0)),
            pl.BlockSpec(keys.shape, lambda i: (0, 0, 0, 0)),
        ],
        out_specs=[
            pl.BlockSpec((PEER_HEADS, nk, tb), lambda i: (0, 0, i)),
            pl.BlockSpec((PEER_HEADS, nk, tb), lambda i: (0, 0, i)),
            pl.BlockSpec((PEER_HEADS, tb), lambda i: (0, i)),
        ],
        out_shape=[
            jax.ShapeDtypeStruct((PEER_HEADS, nk, t), F32),
            jax.ShapeDtypeStruct((PEER_HEADS, nk, t), F32),
            jax.ShapeDtypeStruct((PEER_HEADS, t), F32),
        ],
        compiler_params=_cparams(("parallel",)),
        name="route",
    )(q, keys)


def _peer_kernel(h2t_ref, u_ref, vt_ref, s1_ref, s2_ref, tau_ref, o_ref, at0, at1, gt0, gt1, *, n_eblk, lane_chunk,
                 mxu_chunk):
    p = pl.program_id(0)
    eb = u_ref.shape[0]
    ts = h2t_ref.shape[1]
    n_last = pl.num_programs(0) - 3
    e_b = jnp.clip(p - 1, 0, n_last) % n_eblk
    e_c = jnp.clip(p - 2, 0, n_last) % n_eblk
    tiles = eb // PEER_KEYS
    unit_rows = 64
    row0 = (e_b % (SUBLANES // tiles)) * tiles

    @pl.when(p == 0)
    def _():
        for ref in (at0, at1, gt0, gt1):
            ref[...] = jnp.zeros_like(ref)

    @pl.when(e_c == 0)
    def _():
        o_ref[...] = jnp.zeros_like(o_ref)

    def step(at_a, at_b, gt_b, gt_c):
        for j in range(ts // mxu_chunk):
            js = slice(j * mxu_chunk, (j + 1) * mxu_chunk)
            d_rows = o_ref.shape[0] // tiles
            k_rows = u_ref.shape[1] // tiles
            for it in range(tiles):
                ds_ = slice(it * d_rows, (it + 1) * d_rows)
                ks = slice(it * k_rows, (it + 1) * k_rows)
                o_ref[ds_, js] += jnp.dot(vt_ref[ds_, :], gt_c[:, js], preferred_element_type=F32)
                part = jnp.dot(u_ref[:, ks], h2t_ref[ks, js], preferred_element_type=F32)
                if it == 0:
                    at_a[:, js] = part
                else:
                    at_a[:, js] += part
                s1_rows = [s1_ref[h, pl.ds(row0 + it, 1), js] for h in range(PEER_HEADS)]
                for lc in range(mxu_chunk // lane_chunk):
                    ls = slice(j * mxu_chunk + lc * lane_chunk, j * mxu_chunk + (lc + 1) * lane_chunk)
                    rs = slice(lc * lane_chunk, (lc + 1) * lane_chunk)
                    for rb in range(PEER_KEYS // unit_rows):
                        krows = slice(rb * unit_rows, (rb + 1) * unit_rows)
                        rows = slice(it * PEER_KEYS + rb * unit_rows, it * PEER_KEYS + (rb + 1) * unit_rows)
                        wsum = jnp.zeros((unit_rows, lane_chunk), F32)
                        for h in range(PEER_HEADS):
                            tsum = s1_rows[h][:, rs] + s2_ref[h, krows, ls]
                            wsum = wsum + jnp.where(tsum >= tau_ref[h:h + 1, ls], jnp.exp2(tsum), 0.0)
                        gt_b[rows, ls] = (_gelu(at_b[rows, ls]) * wsum).astype(BF16)

    @pl.when(p % 2 == 0)
    def _():
        step(at0, at1, gt1, gt0)

    @pl.when(p % 2 == 1)
    def _():
        step(at1, at0, gt0, gt1)


def _peer(h2t, u, vt, s1, s2, tau):
    d, t = h2t.shape
    ne = u.shape[0]
    ts = min(1024, t)
    eb = 512
    n_eblk = ne // eb
    n_log = (t // ts) * n_eblk
    rows_per_blk = eb // PEER_KEYS
    assert SUBLANES % rows_per_blk == 0

    def tok(p, lag):
        return jnp.clip(p - lag, 0, n_log - 1) // n_eblk

    def exp(p, lag):
        return jnp.clip(p - lag, 0, n_log - 1) % n_eblk

    return pl.pallas_call(
        functools.partial(_peer_kernel, n_eblk=n_eblk, lane_chunk=min(128, ts), mxu_chunk=min(256, ts)),
        grid=(n_log + 2,),
        in_specs=[
            pl.BlockSpec((d, ts), lambda p: (0, tok(p, 0))),
            pl.BlockSpec((eb, d), lambda p: (exp(p, 0), 0)),
            pl.BlockSpec((d, eb), lambda p: (0, exp(p, 2))),
            pl.BlockSpec((PEER_HEADS, SUBLANES, ts), lambda p: (0, exp(p, 1) * rows_per_blk // SUBLANES, tok(p, 1))),
            pl.BlockSpec((PEER_HEADS, PEER_KEYS, ts), lambda p: (0, 0, tok(p, 1))),
            pl.BlockSpec((PEER_HEADS, ts), lambda p: (0, tok(p, 1))),
        ],
        out_specs=pl.BlockSpec((d, ts), lambda p: (0, tok(p, 2))),
        out_shape=jax.ShapeDtypeStruct((d, t), F32),
        scratch_shapes=[pltpu.VMEM((eb, ts), F32), pltpu.VMEM((eb, ts), F32),
                        pltpu.VMEM((eb, ts), BF16), pltpu.VMEM((eb, ts), BF16)],
        compiler_params=_cparams(("arbitrary",)),
        name="peer",
    )(h2t, u, vt, s1, s2, tau)


def _final_kernel(x1_ref, pt_ref, mod_ref, g_ref, o_ref):
    x2 = x1_ref[...] + mod_ref[5:6, :] * pt_ref[...].T
    o_ref[...] = x2 * lax.rsqrt(jnp.mean(x2 * x2, axis=-1, keepdims=True) + EPS) * g_ref[...]


def _final(x1, peer_t, mod3, g, seq):
    t, d = x1.shape
    tm = min(512, seq)
    per_batch = seq // tm
    return pl.pallas_call(
        _final_kernel,
        grid=(t // tm,),
        in_specs=[
            pl.BlockSpec((tm, d), lambda i: (i, 0)),
            pl.BlockSpec((d, tm), lambda i: (0, i)),
            pl.BlockSpec((None, 6, d), lambda i: (i // per_batch, 0, 0)),
            pl.BlockSpec((1, d), lambda i: (0, 0)),
        ],
        out_specs=pl.BlockSpec((tm, d), lambda i: (i, 0)),
        out_shape=jax.ShapeDtypeStruct((t, d), F32),
        compiler_params=_cparams(("parallel",)),
        name="final",
    )(x1, peer_t, mod3, g.reshape(1, d))


def _layer(x2d, mod3, batch, seq, norm1_g, w_in, lru_conv_w, lru_conv_b, lru_w_a, lru_b_a, lru_w_i, lru_b_i,
           lru_lambda, ssd_conv_w, ssd_conv_b, ssd_dt_bias, ssd_a_log, ssd_d, ssd_norm_g, w_out, norm2_g,
           peer_w_q, peer_sub_keys, peer_u, peer_v):
    d = x2d.shape[1]
    lru_w = lru_lambda.shape[0]
    ssd_w = ssd_norm_g.shape[0]
    xbc_w = ssd_conv_w.shape[1]
    n_heads = ssd_dt_bias.shape[0]
    o_lru_x = lru_w
    o_ssd_z = 2 * lru_w
    o_ssd_xbc = o_ssd_z + ssd_w
    o_ssd_dt = o_ssd_xbc + xbc_w

    wdt = jnp.pad(w_in[:, o_ssd_dt:o_ssd_dt + n_heads], ((0, 0), (0, LANES - n_heads))).astype(BF16)
    wab = jnp.concatenate([lru_w_a, lru_w_i], axis=-1).astype(BF16)
    pad_h = lambda v: jnp.pad(v, (0, LANES - n_heads)).reshape(1, LANES)
    head_of_channel = jnp.arange(ssd_w, dtype=jnp.int32) // SSD_HEAD_DIM
    sel = (jnp.arange(LANES, dtype=jnp.int32)[:, None] == head_of_channel[None, :]).astype(BF16)
    tri = (jnp.arange(SSD_CHUNK)[:, None] >= jnp.arange(SSD_CHUNK)[None, :]).astype(BF16)
    dsk = jnp.repeat(ssd_d, SSD_HEAD_DIM).reshape(1, ssd_w)

    h, dt_raw = _norm1(x2d, mod3, norm1_g, wdt, seq)
    proj = _inproj(h, w_in, o_ssd_dt)
    lru = _lru(proj, lru_conv_w, lru_conv_b.reshape(1, lru_w), wab, lru_b_a.reshape(1, lru_w),
               lru_b_i.reshape(1, lru_w), lru_lambda.reshape(1, lru_w), batch, seq, lru_w, o_lru_x)
    ssd = _ssd(proj, dt_raw, ssd_conv_w, ssd_conv_b.reshape(1, xbc_w), pad_h(ssd_dt_bias), pad_h(ssd_a_log),
               dsk, ssd_norm_g.reshape(1, ssd_w), sel, tri, batch, seq, ssd_w, o_ssd_z, o_ssd_xbc)
    x1, h2t, q = _outproj(lru, ssd, w_out.astype(BF16), x2d, mod3, norm2_g, peer_w_q.astype(BF16), seq)
    s1, s2, tau = _route(q, (peer_sub_keys * math.log2(math.e)).astype(BF16))
    peer_t = _peer(h2t, peer_u.astype(BF16), peer_v.T.astype(BF16), s1, s2, tau)
    return x1, peer_t


def kernel(x, c, w_ada, b_ada, norm1_g, w_in, lru_conv_w, lru_conv_b, lru_w_a, lru_b_a, lru_w_i, lru_b_i, lru_lambda, ssd_conv_w, ssd_conv_b, ssd_dt_bias, ssd_a_log, ssd_d, ssd_norm_g, w_out, norm2_g, peer_w_q, peer_sub_keys, peer_u, peer_v, final_norm_g):
    batch, seq, d = x.shape
    depth = w_ada.shape[0]
    assert depth == 1, "the final residual is fused with the last layer; only depth 1 is wired up"
    x2d = x.reshape(batch * seq, d)
    l = 0
    mod3 = _ada(c, w_ada[l], b_ada[l]).reshape(batch, 6, d)
    x1, peer_t = _layer(x2d, mod3, batch, seq, norm1_g[l], w_in[l], lru_conv_w[l], lru_conv_b[l], lru_w_a[l],
                        lru_b_a[l], lru_w_i[l], lru_b_i[l], lru_lambda[l], ssd_conv_w[l], ssd_conv_b[l],
                        ssd_dt_bias[l], ssd_a_log[l], ssd_d[l], ssd_norm_g[l], w_out[l], norm2_g[l],
                        peer_w_q[l], peer_sub_keys[l], peer_u[l], peer_v[l])
    y = _final(x1, peer_t, mod3, final_norm_g, seq)
    return y.reshape(batch, seq, d)
```

```python
import functools
import math

import jax
import jax.numpy as jnp
from jax import lax
from jax.experimental import pallas as pl
from jax.experimental.pallas import tpu as pltpu

F32 = jnp.float32
BF16 = jnp.bfloat16

EPS = 1e-6
CONV_WIDTH = 4
LRU_C = 8.0
LRU_HEAD_DIM = 128
SSD_HEAD_DIM = 64
SSD_GROUPS = 4
SSD_STATE = 128
SSD_CHUNK = 128
PEER_HEADS = 8
PEER_KEYS = 128
PEER_TOPK = 16
LANES = 128
SUBLANES = 8
NEG_BIG = -1e30
VMEM_LIMIT = 56 * 1024 * 1024


def _cparams(sem, flags=None):
    return pltpu.CompilerParams(dimension_semantics=sem, vmem_limit_bytes=VMEM_LIMIT, flags=flags)


def _split3(x):
    hi = x.astype(BF16)
    r1 = x - hi.astype(F32)
    mid = r1.astype(BF16)
    lo = (r1 - mid.astype(F32)).astype(BF16)
    return hi, mid, lo


def _dot_sel_right(x, sel):
    return sum(jnp.dot(p, sel, preferred_element_type=F32) for p in _split3(x))


def _dot_sel_left(sel, x):
    return sum(jnp.dot(sel, p, preferred_element_type=F32) for p in _split3(x))


def _softplus(x):
    return jnp.maximum(x, 0.0) + jnp.log1p(jnp.exp(-jnp.abs(x)))


def _silu(x):
    return x * jax.nn.sigmoid(x)


def _causal_conv(xpad, cw_ref, cb_ref, rows):
    e = xpad[0:SUBLANES + rows, :]
    z = cw_ref[0:1, :] * e
    for k in range(1, CONV_WIDTH):
        z = pltpu.roll(z, 1, 0) + cw_ref[k:k + 1, :] * e
    return z[SUBLANES:, :] + cb_ref[...]


def _gelu(x):
    return 0.5 * x * (1.0 + jnp.tanh(math.sqrt(2.0 / math.pi) * (x + 0.044715 * (x * x * x))))


def _ada_kernel(c_ref, w_ref, b_ref, o_ref, act_ref):
    nb = c_ref.shape[0]
    tn = w_ref.shape[1]

    @pl.when(pl.program_id(0) == 0)
    def _():
        cv = c_ref[...]
        act_ref[...] = _silu(cv)

    for k in range(tn // LANES):
        sl = slice(k * LANES, (k + 1) * LANES)
        wk = w_ref[:, sl]
        for b in range(nb):
            o_ref[b:b + 1, sl] = jnp.sum(wk * act_ref[b], axis=0, keepdims=True) + b_ref[:, sl]


def _ada(c, w_ada, b_ada):
    nb, d = c.shape
    n = w_ada.shape[1]
    tn = 1024
    cb = jnp.broadcast_to(c[:, :, None], (nb, d, LANES))
    return pl.pallas_call(
        _ada_kernel,
        grid=(n // tn,),
        in_specs=[
            pl.BlockSpec((nb, d, LANES), lambda j: (0, 0, 0)),
            pl.BlockSpec((d, tn), lambda j: (0, j)),
            pl.BlockSpec((1, tn), lambda j: (0, j)),
        ],
        out_specs=pl.BlockSpec((nb, tn), lambda j: (0, j)),
        out_shape=jax.ShapeDtypeStruct((nb, n), F32),
        scratch_shapes=[pltpu.VMEM((nb, d, LANES), F32)],
        compiler_params=_cparams(("arbitrary",)),
        name="ada",
    )(cb, w_ada, b_ada.reshape(1, n))


def _norm1_kernel(x_ref, mod_ref, g_ref, wdt_ref, h_ref, dt_ref):
    x = x_ref[...]
    y = x * lax.rsqrt(jnp.mean(x * x, axis=-1, keepdims=True) + EPS) * g_ref[...]
    h = (y * (1.0 + mod_ref[1:2, :]) + mod_ref[0:1, :]).astype(BF16)
    h_ref[...] = h
    dt_ref[...] = jnp.dot(h, wdt_ref[...], preferred_element_type=F32)


def _norm1(x2d, mod3, g, wdt, seq):
    t, d = x2d.shape
    tm = min(512, seq)
    per_batch = seq // tm
    return pl.pallas_call(
        _norm1_kernel,
        grid=(t // tm,),
        in_specs=[
            pl.BlockSpec((tm, d), lambda i: (i, 0)),
            pl.BlockSpec((None, 6, d), lambda i: (i // per_batch, 0, 0)),
            pl.BlockSpec((1, d), lambda i: (0, 0)),
            pl.BlockSpec((d, LANES), lambda i: (0, 0)),
        ],
        out_specs=[
            pl.BlockSpec((tm, d), lambda i: (i, 0)),
            pl.BlockSpec((tm, LANES), lambda i: (i, 0)),
        ],
        out_shape=[jax.ShapeDtypeStruct((t, d), BF16), jax.ShapeDtypeStruct((t, LANES), F32)],
        compiler_params=_cparams(("parallel",)),
        name="norm1",
    )(x2d, mod3, g.reshape(1, d), wdt)


def _inproj_kernel(h_ref, w_ref, o_ref, wb_ref):
    @pl.when(pl.program_id(1) == 0)
    def _():
        wb_ref[...] = w_ref[...].astype(BF16)

    o_ref[...] = jnp.dot(h_ref[...], wb_ref[...], preferred_element_type=F32)


def _inproj(h, w_in, n_main):
    t, d = h.shape
    tn = 1152
    tm = min(1024, t)
    return pl.pallas_call(
        _inproj_kernel,
        grid=(n_main // tn, t // tm),
        in_specs=[
            pl.BlockSpec((tm, d), lambda j, i: (i, 0)),
            pl.BlockSpec((d, tn), lambda j, i: (0, j)),
        ],
        out_specs=pl.BlockSpec((tm, tn), lambda j, i: (i, j)),
        out_shape=jax.ShapeDtypeStruct((t, n_main), F32),
        scratch_shapes=[pltpu.VMEM((d, tn), BF16)],
        compiler_params=_cparams(("parallel", "arbitrary")),
        name="inproj",
    )(h, w_in)


def _lru_kernel(g_ref, x_ref, cw_ref, cb_ref, wab_ref, ba_ref, bi_ref, lam_ref, o_ref,
                xpad, a_scr, b_scr, hs, hcar):
    tt, c = x_ref.shape
    t = pl.program_id(2)

    @pl.when(t == 0)
    def _():
        xpad[0:SUBLANES, :] = jnp.zeros((SUBLANES, c), F32)
        hcar[...] = jnp.zeros_like(hcar)

    xpad[SUBLANES:SUBLANES + tt, :] = x_ref[...]
    y = _causal_conv(xpad, cw_ref, cb_ref, tt)
    xpad[0:SUBLANES, :] = xpad[tt:tt + SUBLANES, :]

    row = lax.broadcasted_iota(jnp.int32, (tt // SUBLANES, SUBLANES, LRU_HEAD_DIM), 1)
    for j in range(c // LRU_HEAD_DIM):
        sl = slice(j * LRU_HEAD_DIM, (j + 1) * LRU_HEAD_DIM)
        yj = y[:, sl]
        gm = jnp.dot(yj.astype(BF16), wab_ref[j], preferred_element_type=F32)
        r = jax.nn.sigmoid(gm[:, :LRU_HEAD_DIM] + ba_ref[:, sl])
        i = jax.nn.sigmoid(gm[:, LRU_HEAD_DIM:] + bi_ref[:, sl])
        log_a = (-LRU_C * _softplus(-lam_ref[:, sl])) * r
        a = jnp.exp(log_a)
        th = jnp.tanh(log_a)
        b = jnp.sqrt((-2.0 * th) / (1.0 - th)) * (i * yj)
        a = a.reshape(tt // SUBLANES, SUBLANES, LRU_HEAD_DIM)
        b = b.reshape(tt // SUBLANES, SUBLANES, LRU_HEAD_DIM)
        for s in (1, 2, 4):
            a_sh = pltpu.roll(a, s, 1)
            b_sh = pltpu.roll(b, s, 1)
            m = row >= s
            b = jnp.where(m, a * b_sh + b, b)
            a = jnp.where(m, a * a_sh, a)
        a_scr[:, sl] = a.reshape(tt, LRU_HEAD_DIM)
        b_scr[:, sl] = b.reshape(tt, LRU_HEAD_DIM)

    carry = hcar[...]
    for n in range(tt // SUBLANES):
        rs = slice(n * SUBLANES, (n + 1) * SUBLANES)
        hb = b_scr[rs, :] + a_scr[rs, :] * carry
        hs[rs, :] = hb
        carry = hb[SUBLANES - 1:SUBLANES, :]
    hcar[...] = carry
    o_ref[...] = (_gelu(g_ref[...]) * hs[...]).astype(BF16)


def _lru(proj, cw, cb, wab, ba, bi, lam, batch, seq, width, x_col0):
    t = proj.shape[0]
    c = 512
    tt = min(512, seq)
    nt = seq // tt
    ncb = width // c
    hpb = c // LRU_HEAD_DIM
    xoff = x_col0 // c
    return pl.pallas_call(
        _lru_kernel,
        grid=(batch, ncb, nt),
        in_specs=[
            pl.BlockSpec((tt, c), lambda b, j, i: (b * nt + i, j)),
            pl.BlockSpec((tt, c), lambda b, j, i: (b * nt + i, xoff + j)),
            pl.BlockSpec((CONV_WIDTH, c), lambda b, j, i: (0, j)),
            pl.BlockSpec((1, c), lambda b, j, i: (0, j)),
            pl.BlockSpec((hpb, LRU_HEAD_DIM, 2 * LRU_HEAD_DIM), lambda b, j, i: (j, 0, 0)),
            pl.BlockSpec((1, c), lambda b, j, i: (0, j)),
            pl.BlockSpec((1, c), lambda b, j, i: (0, j)),
            pl.BlockSpec((1, c), lambda b, j, i: (0, j)),
        ],
        out_specs=pl.BlockSpec((tt, c), lambda b, j, i: (b * nt + i, j)),
        out_shape=jax.ShapeDtypeStruct((t, width), BF16),
        scratch_shapes=[
            pltpu.VMEM((tt + SUBLANES, c), F32),
            pltpu.VMEM((tt, c), F32),
            pltpu.VMEM((tt, c), F32),
            pltpu.VMEM((tt, c), F32),
            pltpu.VMEM((1, c), F32),
        ],
        compiler_params=_cparams(("parallel", "parallel", "arbitrary")),
        name="lru",
    )(proj, proj, cw, cb, wab, ba, bi, lam)


def _ssd_kernel(z_ref, xbc_ref, dt_ref, cw_ref, cb_ref, dtb_ref, alog_ref, dsk_ref, ng_ref, e_ref, tri_ref,
                o_ref, xpad, st):
    L = SSD_CHUNK
    N = SSD_STATE
    w = z_ref.shape[1]
    gw = w // SSD_GROUPS
    gn = SSD_GROUPS * N

    @pl.when(pl.program_id(1) == 0)
    def _():
        xpad[0:SUBLANES, :] = jnp.zeros((SUBLANES, xpad.shape[1]), F32)
        st[...] = jnp.zeros_like(st)

    xpad[SUBLANES:SUBLANES + L, :] = xbc_ref[...]
    y = _causal_conv(xpad, cw_ref, cb_ref, L)
    xpad[0:SUBLANES, :] = xpad[L:L + SUBLANES, :]
    act = _silu(y)
    xs = act[:, :w]
    bm = act[:, w:w + gn]
    cm = act[:, w + gn:]

    dt = _softplus(dt_ref[...] + dtb_ref[...])
    da = dt * (-jnp.exp(alog_ref[...]))
    a_cs = _dot_sel_left(tri_ref[...], da)
    a_cs_t = a_cs.T
    ea = jnp.exp(a_cs)
    dte = jnp.exp(a_cs[L - 1:L, :] - a_cs)
    sel = e_ref[...]
    dt_f = _dot_sel_right(dt, sel)
    ea_f = _dot_sel_right(ea, sel)
    dte_f = _dot_sel_right(dte, sel)
    xc = xs * dt_f
    xcb = xc.astype(BF16)
    xd = (xc * dte_f).astype(BF16)

    causal = lax.broadcasted_iota(jnp.int32, (L, L), 0) >= lax.broadcasted_iota(jnp.int32, (L, L), 1)
    lane = lax.broadcasted_iota(jnp.int32, (L, LANES), 1)
    low_half = lane < SSD_HEAD_DIM
    heads_per_group = gw // SSD_HEAD_DIM
    for g in range(SSD_GROUPS):
        gs = slice(g * gw, (g + 1) * gw)
        cg = cm[:, g * N:(g + 1) * N].astype(BF16)
        bg = bm[:, g * N:(g + 1) * N]
        cbm = lax.dot_general(cg, bg.astype(BF16), (((1,), (1,)), ((), ())), preferred_element_type=F32)
        bg_t = bg.T.astype(BF16)
        s_prev = st[g]
        y_off = jnp.dot(cg, s_prev.astype(BF16), preferred_element_type=F32) * ea_f[:, gs]
        s_new = jnp.dot(bg_t, xd[:, gs], preferred_element_type=F32)
        st[g] = ea_f[L - 1:L, gs] * s_prev + s_new
        for p in range(heads_per_group // 2):
            h0 = g * heads_per_group + 2 * p
            cs = slice(h0 * SSD_HEAD_DIM, (h0 + 2) * SSD_HEAD_DIM)
            ms = []
            for hh in (h0, h0 + 1):
                seg = a_cs[:, hh:hh + 1] - a_cs_t[hh:hh + 1, :]
                ms.append(cbm * jnp.exp(jnp.where(causal, seg, NEG_BIG)))
            lhs = jnp.concatenate(ms, axis=1).astype(BF16)
            pair = xcb[:, cs]
            zero = jnp.zeros_like(pair)
            rhs = jnp.concatenate([jnp.where(low_half, pair, zero), jnp.where(low_half, zero, pair)], axis=0)
            y_diag = jnp.dot(lhs, rhs, preferred_element_type=F32)
            yp = y_diag + y_off[:, 2 * p * SSD_HEAD_DIM:(2 * p + 2) * SSD_HEAD_DIM] + dsk_ref[:, cs] * xs[:, cs]
            xpad[SUBLANES:SUBLANES + L, cs] = yp

    yz = xpad[SUBLANES:SUBLANES + L, 0:w] * _silu(z_ref[...])
    for g in range(SSD_GROUPS):
        gs = slice(g * gw, (g + 1) * gw)
        v = yz[:, gs]
        v = v * lax.rsqrt(jnp.mean(v * v, axis=-1, keepdims=True) + EPS)
        o_ref[:, gs] = (v * ng_ref[:, gs]).astype(BF16)


def _ssd(proj, dt_raw, cw, cb, dtb, alog, dsk, ng, sel, tri, batch, seq, width, z_col0, xbc_col0):
    t = proj.shape[0]
    xw = cw.shape[1]
    L = SSD_CHUNK
    nc = seq // L
    zoff = z_col0 // width
    xoff = xbc_col0 // xw
    return pl.pallas_call(
        _ssd_kernel,
        grid=(batch, nc),
        in_specs=[
            pl.BlockSpec((L, width), lambda b, i: (b * nc + i, zoff)),
            pl.BlockSpec((L, xw), lambda b, i: (b * nc + i, xoff)),
            pl.BlockSpec((L, LANES), lambda b, i: (b * nc + i, 0)),
            pl.BlockSpec((CONV_WIDTH, xw), lambda b, i: (0, 0)),
            pl.BlockSpec((1, xw), lambda b, i: (0, 0)),
            pl.BlockSpec((1, LANES), lambda b, i: (0, 0)),
            pl.BlockSpec((1, LANES), lambda b, i: (0, 0)),
            pl.BlockSpec((1, width), lambda b, i: (0, 0)),
            pl.BlockSpec((1, width), lambda b, i: (0, 0)),
            pl.BlockSpec((LANES, width), lambda b, i: (0, 0)),
            pl.BlockSpec((L, L), lambda b, i: (0, 0)),
        ],
        out_specs=pl.BlockSpec((L, width), lambda b, i: (b * nc + i, 0)),
        out_shape=jax.ShapeDtypeStruct((t, width), BF16),
        scratch_shapes=[
            pltpu.VMEM((L + SUBLANES, xw), F32),
            pltpu.VMEM((SSD_GROUPS, SSD_STATE, width // SSD_GROUPS), F32),
        ],
        compiler_params=_cparams(("parallel", "arbitrary")),
        name="ssd",
    )(proj, proj, dt_raw, cw, cb, dtb, alog, dsk, ng, sel, tri)


def _outproj_kernel(lru_ref, ssd_ref, wo_ref, x_ref, mod_ref, g_ref, wq_ref, x1_ref, h2t_ref, q_ref, acc):
    k = pl.program_id(1)
    nk = pl.num_programs(1)

    @pl.when(k == 0)
    def _():
        acc[...] = jnp.zeros_like(acc)

    @pl.when(k < nk // 2)
    def _():
        acc[...] += jnp.dot(lru_ref[...], wo_ref[...], preferred_element_type=F32)

    @pl.when(k >= nk // 2)
    def _():
        acc[...] += jnp.dot(ssd_ref[...], wo_ref[...], preferred_element_type=F32)

    @pl.when(k == nk - 1)
    def _():
        x1 = x_ref[...] + mod_ref[2:3, :] * acc[...]
        x1_ref[...] = x1
        y = x1 * lax.rsqrt(jnp.mean(x1 * x1, axis=-1, keepdims=True) + EPS) * g_ref[...]
        h2 = y * (1.0 + mod_ref[4:5, :]) + mod_ref[3:4, :]
        h2t_ref[...] = h2.T.astype(BF16)
        q_ref[...] = jnp.dot(h2.astype(BF16), wq_ref[...], preferred_element_type=F32)


def _outproj(lru, ssd, wo, x2d, mod3, g, wq, seq):
    t, d = x2d.shape
    half = lru.shape[1]
    tm = min(512, seq)
    tk = 1024
    nkh = half // tk
    per_batch = seq // tm
    nq = wq.shape[1]
    return pl.pallas_call(
        _outproj_kernel,
        grid=(t // tm, 2 * nkh),
        in_specs=[
            pl.BlockSpec((tm, tk), lambda i, k: (i, jnp.minimum(k, nkh - 1))),
            pl.BlockSpec((tm, tk), lambda i, k: (i, jnp.maximum(k - nkh, 0))),
            pl.BlockSpec((tk, d), lambda i, k: (k, 0)),
            pl.BlockSpec((tm, d), lambda i, k: (i, 0)),
            pl.BlockSpec((None, 6, d), lambda i, k: (i // per_batch, 0, 0)),
            pl.BlockSpec((1, d), lambda i, k: (0, 0)),
            pl.BlockSpec((d, nq), lambda i, k: (0, 0)),
        ],
        out_specs=[
            pl.BlockSpec((tm, d), lambda i, k: (i, 0)),
            pl.BlockSpec((d, tm), lambda i, k: (0, i)),
            pl.BlockSpec((tm, nq), lambda i, k: (i, 0)),
        ],
        out_shape=[
            jax.ShapeDtypeStruct((t, d), F32),
            jax.ShapeDtypeStruct((d, t), BF16),
            jax.ShapeDtypeStruct((t, nq), F32),
        ],
        scratch_shapes=[pltpu.VMEM((tm, d), F32)],
        compiler_params=_cparams(("parallel", "arbitrary")),
        name="outproj",
    )(lru, ssd, wo, x2d, mod3, g.reshape(1, d), wq)


def _top_values(s, count):
    rid = lax.broadcasted_iota(jnp.int32, (count, s.shape[1]), 0)
    out = jnp.zeros((count, s.shape[1]), F32)
    for r in range(count):
        m = jnp.max(s, axis=0, keepdims=True)
        out = jnp.where(rid == r, m, out)
        s = jnp.where(s == m, NEG_BIG, s)
    return out


def _pair_candidates(v1, v2):
    parts = [v1[0:1] + v2]
    parts += [v1[a:a + 1] + v2[0:8] for a in (1, 2, 3)]
    parts += [v1[0:8] + v2[b:b + 1] for b in (0, 1, 2)]
    parts.append(v1[8:16] + v2[0:1])
    return jnp.concatenate(parts, axis=0)


def _kth_and_expsum(cand, count):
    z = jnp.zeros((1, cand.shape[1]), F32)
    m = None
    for _ in range(count):
        m = jnp.max(cand, axis=0, keepdims=True)
        z = z + jnp.exp2(m)
        cand = jnp.where(cand == m, NEG_BIG, cand)
    return m, z


def _route_kernel(q_ref, keys_ref, s1_ref, s2_ref, tau_ref):
    for h in range(PEER_HEADS):
        ss = []
        for half in range(2):
            col = (2 * h + half) * LANES
            qh = q_ref[:, col:col + LANES].astype(BF16)
            s = lax.dot_general(keys_ref[h, half], qh, (((1,), (1,)), ((), ())), preferred_element_type=F32)
            ss.append(s - jnp.max(s, axis=0, keepdims=True))
        s1, s2 = ss
        v1 = _top_values(s1, PEER_TOPK)
        v2 = _top_values(s2, PEER_TOPK)
        _, z = _kth_and_expsum(_pair_candidates(v1, v2), PEER_TOPK)
        lz = jnp.log2(z)
        s1 = s1 - lz
        tau, _ = _kth_and_expsum(_pair_candidates(v1 - lz, v2), PEER_TOPK)
        s1_ref[h] = s1
        s2_ref[h] = s2
        tau_ref[h:h + 1, :] = tau


def _route(q, keys):
    t, nq = q.shape
    tb = min(512, t)
    nk = keys.shape[2]
    return pl.pallas_call(
        _route_kernel,
        grid=(t // tb,),
        in_specs=[
            pl.BlockSpec((tb, nq), lambda i: (i, 0)),
            pl.BlockSpec(keys.shape, lambda i: (0, 0, 0, 0)),
        ],
        out_specs=[
            pl.BlockSpec((PEER_HEADS, nk, tb), lambda i: (0, 0, i)),
            pl.BlockSpec((PEER_HEADS, nk, tb), lambda i: (0, 0, i)),
            pl.BlockSpec((PEER_HEADS, tb), lambda i: (0, i)),
        ],
        out_shape=[
            jax.ShapeDtypeStruct((PEER_HEADS, nk, t), F32),
            jax.ShapeDtypeStruct((PEER_HEADS, nk, t), F32),
            jax.ShapeDtypeStruct((PEER_HEADS, t), F32),
        ],
        compiler_params=_cparams(("parallel",)),
        name="route",
    )(q, keys)


def _vt_kernel(v_ref, o_ref):
    o_ref[...] = v_ref[...].T.astype(BF16)


def _transpose_bf16(v):
    ne, d = v.shape
    eb = 512
    return pl.pallas_call(
        _vt_kernel,
        grid=(ne // eb,),
        in_specs=[pl.BlockSpec((eb, d), lambda e: (e, 0))],
        out_specs=pl.BlockSpec((d, eb), lambda e: (0, e)),
        out_shape=jax.ShapeDtypeStruct((d, ne), BF16),
        compiler_params=_cparams(("parallel",)),
        name="vt",
    )(v)


def _peer_kernel(h2t_ref, u_ref, vt_ref, s1_ref, s2_ref, tau_ref, o_ref, at_scr, gt_scr, *, tc):
    e = pl.program_id(1)
    ts = h2t_ref.shape[1]
    eb = u_ref.shape[0]

    @pl.when(e == 0)
    def _():
        o_ref[...] = jnp.zeros_like(o_ref)

    for c in range(ts // tc):
        cs = slice(c * tc, (c + 1) * tc)
        at_scr[...] = jnp.dot(u_ref[...], h2t_ref[:, cs], preferred_element_type=F32)

        def tile(it, carry):
            rows = pl.ds(pl.multiple_of(it * PEER_KEYS, PEER_KEYS), PEER_KEYS)
            wsum = jnp.zeros((PEER_KEYS, tc), F32)
            for h in range(PEER_HEADS):
                tsum = s1_ref[h, pl.ds(it, 1), cs] + s2_ref[h, :, cs]
                wsum = wsum + jnp.where(tsum >= tau_ref[h:h + 1, cs], jnp.exp2(tsum), 0.0)
            gt_scr[rows, :] = (_gelu(at_scr[rows, :]) * wsum).astype(BF16)
            return carry

        lax.fori_loop(0, eb // PEER_KEYS, tile, 0)
        o_ref[:, cs] += jnp.dot(vt_ref[...], gt_scr[...], preferred_element_type=F32)


def _peer(h2t, u, vt, s1, s2, tau):
    d, t = h2t.shape
    ne = u.shape[0]
    ts = min(1024, t)
    tc = min(512, ts)
    eb = 1024
    return pl.pallas_call(
        functools.partial(_peer_kernel, tc=tc),
        grid=(t // ts, ne // eb),
        in_specs=[
            pl.BlockSpec((d, ts), lambda s, e: (0, s)),
            pl.BlockSpec((eb, d), lambda s, e: (e, 0)),
            pl.BlockSpec((d, eb), lambda s, e: (0, e)),
            pl.BlockSpec((PEER_HEADS, eb // PEER_KEYS, ts), lambda s, e: (0, e, s)),
            pl.BlockSpec((PEER_HEADS, PEER_KEYS, ts), lambda s, e: (0, 0, s)),
            pl.BlockSpec((PEER_HEADS, ts), lambda s, e: (0, s)),
        ],
        out_specs=pl.BlockSpec((d, ts), lambda s, e: (0, s)),
        out_shape=jax.ShapeDtypeStruct((d, t), F32),
        scratch_shapes=[pltpu.VMEM((eb, tc), F32), pltpu.VMEM((eb, tc), BF16)],
        compiler_params=_cparams(("parallel", "arbitrary")),
        name="peer",
    )(h2t, u, vt, s1, s2, tau)


def _final_kernel(x1_ref, pt_ref, mod_ref, g_ref, o_ref):
    x2 = x1_ref[...] + mod_ref[5:6, :] * pt_ref[...].T
    o_ref[...] = x2 * lax.rsqrt(jnp.mean(x2 * x2, axis=-1, keepdims=True) + EPS) * g_ref[...]


def _final(x1, peer_t, mod3, g, seq):
    t, d = x1.shape
    tm = min(512, seq)
    per_batch = seq // tm
    return pl.pallas_call(
        _final_kernel,
        grid=(t // tm,),
        in_specs=[
            pl.BlockSpec((tm, d), lambda i: (i, 0)),
            pl.BlockSpec((d, tm), lambda i: (0, i)),
            pl.BlockSpec((None, 6, d), lambda i: (i // per_batch, 0, 0)),
            pl.BlockSpec((1, d), lambda i: (0, 0)),
        ],
        out_specs=pl.BlockSpec((tm, d), lambda i: (i, 0)),
        out_shape=jax.ShapeDtypeStruct((t, d), F32),
        compiler_params=_cparams(("parallel",)),
        name="final",
    )(x1, peer_t, mod3, g.reshape(1, d))


def _layer(x2d, mod3, batch, seq, norm1_g, w_in, lru_conv_w, lru_conv_b, lru_w_a, lru_b_a, lru_w_i, lru_b_i,
           lru_lambda, ssd_conv_w, ssd_conv_b, ssd_dt_bias, ssd_a_log, ssd_d, ssd_norm_g, w_out, norm2_g,
           peer_w_q, peer_sub_keys, peer_u, peer_v):
    d = x2d.shape[1]
    lru_w = lru_lambda.shape[0]
    ssd_w = ssd_norm_g.shape[0]
    xbc_w = ssd_conv_w.shape[1]
    n_heads = ssd_dt_bias.shape[0]
    o_lru_x = lru_w
    o_ssd_z = 2 * lru_w
    o_ssd_xbc = o_ssd_z + ssd_w
    o_ssd_dt = o_ssd_xbc + xbc_w

    wdt = jnp.pad(w_in[:, o_ssd_dt:o_ssd_dt + n_heads], ((0, 0), (0, LANES - n_heads))).astype(BF16)
    wab = jnp.concatenate([lru_w_a, lru_w_i], axis=-1).astype(BF16)
    pad_h = lambda v: jnp.pad(v, (0, LANES - n_heads)).reshape(1, LANES)
    head_of_channel = jnp.arange(ssd_w, dtype=jnp.int32) // SSD_HEAD_DIM
    sel = (jnp.arange(LANES, dtype=jnp.int32)[:, None] == head_of_channel[None, :]).astype(BF16)
    tri = (jnp.arange(SSD_CHUNK)[:, None] >= jnp.arange(SSD_CHUNK)[None, :]).astype(BF16)
    dsk = jnp.repeat(ssd_d, SSD_HEAD_DIM).reshape(1, ssd_w)

    h, dt_raw = _norm1(x2d, mod3, norm1_g, wdt, seq)
    proj = _inproj(h, w_in, o_ssd_dt)
    lru = _lru(proj, lru_conv_w, lru_conv_b.reshape(1, lru_w), wab, lru_b_a.reshape(1, lru_w),
               lru_b_i.reshape(1, lru_w), lru_lambda.reshape(1, lru_w), batch, seq, lru_w, o_lru_x)
    ssd = _ssd(proj, dt_raw, ssd_conv_w, ssd_conv_b.reshape(1, xbc_w), pad_h(ssd_dt_bias), pad_h(ssd_a_log),
               dsk, ssd_norm_g.reshape(1, ssd_w), sel, tri, batch, seq, ssd_w, o_ssd_z, o_ssd_xbc)
    x1, h2t, q = _outproj(lru, ssd, w_out.astype(BF16), x2d, mod3, norm2_g, peer_w_q.astype(BF16), seq)
    s1, s2, tau = _route(q, (peer_sub_keys * math.log2(math.e)).astype(BF16))
    peer_t = _peer(h2t, peer_u.astype(BF16), _transpose_bf16(peer_v), s1, s2, tau)
    return x1, peer_t


def kernel(x, c, w_ada, b_ada, norm1_g, w_in, lru_conv_w, lru_conv_b, lru_w_a, lru_b_a, lru_w_i, lru_b_i, lru_lambda, ssd_conv_w, ssd_conv_b, ssd_dt_bias, ssd_a_log, ssd_d, ssd_norm_g, w_out, norm2_g, peer_w_q, peer_sub_keys, peer_u, peer_v, final_norm_g):
    batch, seq, d = x.shape
    depth = w_ada.shape[0]
    assert depth == 1, "the final residual is fused with the last layer; only depth 1 is wired up"
    x2d = x.reshape(batch * seq, d)
    l = 0
    mod3 = _ada(c, w_ada[l], b_ada[l]).reshape(batch, 6, d)
    x1, peer_t = _layer(x2d, mod3, batch, seq, norm1_g[l], w_in[l], lru_conv_w[l], lru_conv_b[l], lru_w_a[l],
                        lru_b_a[l], lru_w_i[l], lru_b_i[l], lru_lambda[l], ssd_conv_w[l], ssd_conv_b[l],
                        ssd_dt_bias[l], ssd_a_log[l], ssd_d[l], ssd_norm_g[l], w_out[l], norm2_g[l],
                        peer_w_q[l], peer_sub_keys[l], peer_u[l], peer_v[l])
    y = _final(x1, peer_t, mod3, final_norm_g, seq)
    return y.reshape(batch, seq, d)
```

```python
import functools
import math

import jax
import jax.numpy as jnp
from jax import lax
from jax.experimental import pallas as pl
from jax.experimental.pallas import tpu as pltpu

F32 = jnp.float32
BF16 = jnp.bfloat16

EPS = 1e-6
CONV_WIDTH = 4
LRU_C = 8.0
LRU_HEAD_DIM = 128
SSD_HEAD_DIM = 64
SSD_GROUPS = 4
SSD_STATE = 128
SSD_CHUNK = 128
PEER_HEADS = 8
PEER_KEYS = 128
PEER_TOPK = 16
LANES = 128
SUBLANES = 8
NEG_BIG = -1e30
VMEM_LIMIT = 56 * 1024 * 1024


def _cparams(sem, flags=None):
    return pltpu.CompilerParams(dimension_semantics=sem, vmem_limit_bytes=VMEM_LIMIT, flags=flags)


def _split3(x):
    hi = x.astype(BF16)
    r1 = x - hi.astype(F32)
    mid = r1.astype(BF16)
    lo = (r1 - mid.astype(F32)).astype(BF16)
    return hi, mid, lo


def _dot_sel_right(x, sel):
    return sum(jnp.dot(p, sel, preferred_element_type=F32) for p in _split3(x))


def _dot_sel_left(sel, x):
    return sum(jnp.dot(sel, p, preferred_element_type=F32) for p in _split3(x))


def _softplus(x):
    return jnp.maximum(x, 0.0) + jnp.log1p(jnp.exp(-jnp.abs(x)))


def _silu(x):
    return x * jax.nn.sigmoid(x)


def _causal_conv(xpad, cw_ref, cb_ref, rows):
    e = xpad[0:SUBLANES + rows, :]
    z = cw_ref[0:1, :] * e
    for k in range(1, CONV_WIDTH):
        z = pltpu.roll(z, 1, 0) + cw_ref[k:k + 1, :] * e
    return z[SUBLANES:, :] + cb_ref[...]


def _gelu(x):
    c = math.sqrt(2.0 / math.pi)
    hx = 0.5 * x
    return hx + hx * jnp.tanh(x * (c + (c * 0.044715) * (x * x)))


def _ada_kernel(c_ref, w_ref, b_ref, o_ref, act_ref):
    nb = c_ref.shape[0]
    tn = w_ref.shape[1]

    @pl.when(pl.program_id(0) == 0)
    def _():
        cv = c_ref[...]
        act_ref[...] = _silu(cv)

    for k in range(tn // LANES):
        sl = slice(k * LANES, (k + 1) * LANES)
        wk = w_ref[:, sl]
        for b in range(nb):
            o_ref[b:b + 1, sl] = jnp.sum(wk * act_ref[b], axis=0, keepdims=True) + b_ref[:, sl]


def _ada(c, w_ada, b_ada):
    nb, d = c.shape
    n = w_ada.shape[1]
    tn = 1024
    cb = jnp.broadcast_to(c[:, :, None], (nb, d, LANES))
    return pl.pallas_call(
        _ada_kernel,
        grid=(n // tn,),
        in_specs=[
            pl.BlockSpec((nb, d, LANES), lambda j: (0, 0, 0)),
            pl.BlockSpec((d, tn), lambda j: (0, j)),
            pl.BlockSpec((1, tn), lambda j: (0, j)),
        ],
        out_specs=pl.BlockSpec((nb, tn), lambda j: (0, j)),
        out_shape=jax.ShapeDtypeStruct((nb, n), F32),
        scratch_shapes=[pltpu.VMEM((nb, d, LANES), F32)],
        compiler_params=_cparams(("arbitrary",)),
        name="ada",
    )(cb, w_ada, b_ada.reshape(1, n))


def _norm1_kernel(x_ref, mod_ref, g_ref, wdt_ref, h_ref, dt_ref):
    x = x_ref[...]
    y = x * lax.rsqrt(jnp.mean(x * x, axis=-1, keepdims=True) + EPS) * g_ref[...]
    h = (y * (1.0 + mod_ref[1:2, :]) + mod_ref[0:1, :]).astype(BF16)
    h_ref[...] = h
    dt_ref[...] = jnp.dot(h, wdt_ref[...], preferred_element_type=F32)


def _norm1(x2d, mod3, g, wdt, seq):
    t, d = x2d.shape
    tm = min(512, seq)
    per_batch = seq // tm
    return pl.pallas_call(
        _norm1_kernel,
        grid=(t // tm,),
        in_specs=[
            pl.BlockSpec((tm, d), lambda i: (i, 0)),
            pl.BlockSpec((None, 6, d), lambda i: (i // per_batch, 0, 0)),
            pl.BlockSpec((1, d), lambda i: (0, 0)),
            pl.BlockSpec((d, LANES), lambda i: (0, 0)),
        ],
        out_specs=[
            pl.BlockSpec((tm, d), lambda i: (i, 0)),
            pl.BlockSpec((tm, LANES), lambda i: (i, 0)),
        ],
        out_shape=[jax.ShapeDtypeStruct((t, d), BF16), jax.ShapeDtypeStruct((t, LANES), F32)],
        compiler_params=_cparams(("parallel",)),
        name="norm1",
    )(x2d, mod3, g.reshape(1, d), wdt)


def _inproj_kernel(h_ref, w_ref, o_ref):
    o_ref[...] = jnp.dot(h_ref[...], w_ref[...], preferred_element_type=F32)


def _inproj(h, w_main):
    t, d = h.shape
    n_main = w_main.shape[1]
    tn = 1152
    tm = min(2048, t)
    return pl.pallas_call(
        _inproj_kernel,
        grid=(n_main // tn, t // tm),
        in_specs=[
            pl.BlockSpec((tm, d), lambda j, i: (i, 0)),
            pl.BlockSpec((d, tn), lambda j, i: (0, j)),
        ],
        out_specs=pl.BlockSpec((tm, tn), lambda j, i: (i, j)),
        out_shape=jax.ShapeDtypeStruct((t, n_main), F32),
        compiler_params=_cparams(("parallel", "arbitrary")),
        name="inproj",
    )(h, w_main)


def _lru_kernel(g_ref, x_ref, cw_ref, cb_ref, wab_ref, ba_ref, bi_ref, lam_ref, o_ref,
                xpad, a_scr, b_scr, hs, hcar):
    tt, c = x_ref.shape
    t = pl.program_id(2)

    @pl.when(t == 0)
    def _():
        xpad[0:SUBLANES, :] = jnp.zeros((SUBLANES, c), F32)
        hcar[...] = jnp.zeros_like(hcar)

    xpad[SUBLANES:SUBLANES + tt, :] = x_ref[...]
    y = _causal_conv(xpad, cw_ref, cb_ref, tt)
    xpad[0:SUBLANES, :] = xpad[tt:tt + SUBLANES, :]

    row = lax.broadcasted_iota(jnp.int32, (tt // SUBLANES, SUBLANES, LRU_HEAD_DIM), 1)
    for j in range(c // LRU_HEAD_DIM):
        sl = slice(j * LRU_HEAD_DIM, (j + 1) * LRU_HEAD_DIM)
        yj = y[:, sl]
        gm = jnp.dot(yj.astype(BF16), wab_ref[j], preferred_element_type=F32)
        r = jax.nn.sigmoid(gm[:, :LRU_HEAD_DIM] + ba_ref[:, sl])
        i = jax.nn.sigmoid(gm[:, LRU_HEAD_DIM:] + bi_ref[:, sl])
        log_a = (-LRU_C * _softplus(-lam_ref[:, sl])) * r
        a = jnp.exp(log_a)
        th = jnp.tanh(log_a)
        b = jnp.sqrt((-2.0 * th) / (1.0 - th)) * (i * yj)
        a = a.reshape(tt // SUBLANES, SUBLANES, LRU_HEAD_DIM)
        b = b.reshape(tt // SUBLANES, SUBLANES, LRU_HEAD_DIM)
        for s in (1, 2, 4):
            a_sh = pltpu.roll(a, s, 1)
            b_sh = pltpu.roll(b, s, 1)
            m = row >= s
            b = jnp.where(m, a * b_sh + b, b)
            a = jnp.where(m, a * a_sh, a)
        a_scr[:, sl] = a.reshape(tt, LRU_HEAD_DIM)
        b_scr[:, sl] = b.reshape(tt, LRU_HEAD_DIM)

    carry = hcar[...]
    for n in range(tt // SUBLANES):
        rs = slice(n * SUBLANES, (n + 1) * SUBLANES)
        hb = b_scr[rs, :] + a_scr[rs, :] * carry
        hs[rs, :] = hb
        carry = hb[SUBLANES - 1:SUBLANES, :]
    hcar[...] = carry
    o_ref[...] = (_gelu(g_ref[...]) * hs[...]).astype(BF16)


def _lru(proj, cw, cb, wab, ba, bi, lam, batch, seq, width, x_col0):
    t = proj.shape[0]
    c = 512
    tt = min(512, seq)
    nt = seq // tt
    ncb = width // c
    hpb = c // LRU_HEAD_DIM
    xoff = x_col0 // c
    return pl.pallas_call(
        _lru_kernel,
        grid=(batch, ncb, nt),
        in_specs=[
            pl.BlockSpec((tt, c), lambda b, j, i: (b * nt + i, j)),
            pl.BlockSpec((tt, c), lambda b, j, i: (b * nt + i, xoff + j)),
            pl.BlockSpec((CONV_WIDTH, c), lambda b, j, i: (0, j)),
            pl.BlockSpec((1, c), lambda b, j, i: (0, j)),
            pl.BlockSpec((hpb, LRU_HEAD_DIM, 2 * LRU_HEAD_DIM), lambda b, j, i: (j, 0, 0)),
            pl.BlockSpec((1, c), lambda b, j, i: (0, j)),
            pl.BlockSpec((1, c), lambda b, j, i: (0, j)),
            pl.BlockSpec((1, c), lambda b, j, i: (0, j)),
        ],
        out_specs=pl.BlockSpec((tt, c), lambda b, j, i: (b * nt + i, j)),
        out_shape=jax.ShapeDtypeStruct((t, width), BF16),
        scratch_shapes=[
            pltpu.VMEM((tt + SUBLANES, c), F32),
            pltpu.VMEM((tt, c), F32),
            pltpu.VMEM((tt, c), F32),
            pltpu.VMEM((tt, c), F32),
            pltpu.VMEM((1, c), F32),
        ],
        compiler_params=_cparams(("parallel", "parallel", "arbitrary")),
        name="lru",
    )(proj, proj, cw, cb, wab, ba, bi, lam)


def _ssd_kernel(z_ref, xbc_ref, dt_ref, cw_ref, cb_ref, dtb_ref, alog_ref, dsk_ref, ng_ref, e_ref, tri_ref,
                o_ref, xpad, st):
    L = SSD_CHUNK
    N = SSD_STATE
    w = z_ref.shape[1]
    gw = w // SSD_GROUPS
    gn = SSD_GROUPS * N

    @pl.when(pl.program_id(1) == 0)
    def _():
        xpad[0:SUBLANES, :] = jnp.zeros((SUBLANES, xpad.shape[1]), F32)
        st[...] = jnp.zeros_like(st)

    xpad[SUBLANES:SUBLANES + L, :] = xbc_ref[...]
    y = _causal_conv(xpad, cw_ref, cb_ref, L)
    xpad[0:SUBLANES, :] = xpad[L:L + SUBLANES, :]
    act = _silu(y)
    xs = act[:, :w]
    bm = act[:, w:w + gn]
    cm = act[:, w + gn:]

    dt = _softplus(dt_ref[...] + dtb_ref[...])
    da = dt * (-jnp.exp(alog_ref[...]))
    a_cs = _dot_sel_left(tri_ref[...], da)
    a_cs_t = a_cs.T
    ea = jnp.exp(a_cs)
    dte = jnp.exp(a_cs[L - 1:L, :] - a_cs)
    sel = e_ref[...]
    dt_f = _dot_sel_right(dt, sel)
    ea_f = _dot_sel_right(ea, sel)
    dte_f = _dot_sel_right(dte, sel)
    xc = xs * dt_f
    xcb = xc.astype(BF16)
    xd = (xc * dte_f).astype(BF16)

    causal = lax.broadcasted_iota(jnp.int32, (L, L), 0) >= lax.broadcasted_iota(jnp.int32, (L, L), 1)
    lane = lax.broadcasted_iota(jnp.int32, (L, LANES), 1)
    low_half = lane < SSD_HEAD_DIM
    heads_per_group = gw // SSD_HEAD_DIM
    for g in range(SSD_GROUPS):
        gs = slice(g * gw, (g + 1) * gw)
        cg = cm[:, g * N:(g + 1) * N].astype(BF16)
        bg = bm[:, g * N:(g + 1) * N]
        cbm = lax.dot_general(cg, bg.astype(BF16), (((1,), (1,)), ((), ())), preferred_element_type=F32)
        bg_t = bg.T.astype(BF16)
        s_prev = st[g]
        y_off = jnp.dot(cg, s_prev.astype(BF16), preferred_element_type=F32) * ea_f[:, gs]
        s_new = jnp.dot(bg_t, xd[:, gs], preferred_element_type=F32)
        st[g] = ea_f[L - 1:L, gs] * s_prev + s_new
        for p in range(heads_per_group // 2):
            h0 = g * heads_per_group + 2 * p
            cs = slice(h0 * SSD_HEAD_DIM, (h0 + 2) * SSD_HEAD_DIM)
            ms = []
            for hh in (h0, h0 + 1):
                seg = a_cs[:, hh:hh + 1] - a_cs_t[hh:hh + 1, :]
                ms.append(cbm * jnp.exp(jnp.where(causal, seg, NEG_BIG)))
            lhs = jnp.concatenate(ms, axis=1).astype(BF16)
            pair = xcb[:, cs]
            zero = jnp.zeros_like(pair)
            rhs = jnp.concatenate([jnp.where(low_half, pair, zero), jnp.where(low_half, zero, pair)], axis=0)
            y_diag = jnp.dot(lhs, rhs, preferred_element_type=F32)
            yp = y_diag + y_off[:, 2 * p * SSD_HEAD_DIM:(2 * p + 2) * SSD_HEAD_DIM] + dsk_ref[:, cs] * xs[:, cs]
            xpad[SUBLANES:SUBLANES + L, cs] = yp

    yz = xpad[SUBLANES:SUBLANES + L, 0:w] * _silu(z_ref[...])
    for g in range(SSD_GROUPS):
        gs = slice(g * gw, (g + 1) * gw)
        v = yz[:, gs]
        v = v * lax.rsqrt(jnp.mean(v * v, axis=-1, keepdims=True) + EPS)
        o_ref[:, gs] = (v * ng_ref[:, gs]).astype(BF16)


def _ssd(proj, dt_raw, cw, cb, dtb, alog, dsk, ng, sel, tri, batch, seq, width, z_col0, xbc_col0):
    t = proj.shape[0]
    xw = cw.shape[1]
    L = SSD_CHUNK
    nc = seq // L
    zoff = z_col0 // width
    xoff = xbc_col0 // xw
    return pl.pallas_call(
        _ssd_kernel,
        grid=(batch, nc),
        in_specs=[
            pl.BlockSpec((L, width), lambda b, i: (b * nc + i, zoff)),
            pl.BlockSpec((L, xw), lambda b, i: (b * nc + i, xoff)),
            pl.BlockSpec((L, LANES), lambda b, i: (b * nc + i, 0)),
            pl.BlockSpec((CONV_WIDTH, xw), lambda b, i: (0, 0)),
            pl.BlockSpec((1, xw), lambda b, i: (0, 0)),
            pl.BlockSpec((1, LANES), lambda b, i: (0, 0)),
            pl.BlockSpec((1, LANES), lambda b, i: (0, 0)),
            pl.BlockSpec((1, width), lambda b, i: (0, 0)),
            pl.BlockSpec((1, width), lambda b, i: (0, 0)),
            pl.BlockSpec((LANES, width), lambda b, i: (0, 0)),
            pl.BlockSpec((L, L), lambda b, i: (0, 0)),
        ],
        out_specs=pl.BlockSpec((L, width), lambda b, i: (b * nc + i, 0)),
        out_shape=jax.ShapeDtypeStruct((t, width), BF16),
        scratch_shapes=[
            pltpu.VMEM((L + SUBLANES, xw), F32),
            pltpu.VMEM((SSD_GROUPS, SSD_STATE, width // SSD_GROUPS), F32),
        ],
        compiler_params=_cparams(("parallel", "arbitrary")),
        name="ssd",
    )(proj, proj, dt_raw, cw, cb, dtb, alog, dsk, ng, sel, tri)


def _outproj_kernel(lru_ref, ssd_ref, wo_ref, x_ref, mod_ref, g_ref, wq_ref, x1_ref, h2t_ref, q_ref, acc):
    k = pl.program_id(1)
    nk = pl.num_programs(1)

    @pl.when(k == 0)
    def _():
        acc[...] = jnp.zeros_like(acc)

    @pl.when(k < nk // 2)
    def _():
        acc[...] += jnp.dot(lru_ref[...], wo_ref[...], preferred_element_type=F32)

    @pl.when(k >= nk // 2)
    def _():
        acc[...] += jnp.dot(ssd_ref[...], wo_ref[...], preferred_element_type=F32)

    @pl.when(k == nk - 1)
    def _():
        x1 = x_ref[...] + mod_ref[2:3, :] * acc[...]
        x1_ref[...] = x1
        y = x1 * lax.rsqrt(jnp.mean(x1 * x1, axis=-1, keepdims=True) + EPS) * g_ref[...]
        h2 = y * (1.0 + mod_ref[4:5, :]) + mod_ref[3:4, :]
        h2t_ref[...] = h2.T.astype(BF16)
        q_ref[...] = jnp.dot(h2.astype(BF16), wq_ref[...], preferred_element_type=F32)


def _outproj(lru, ssd, wo, x2d, mod3, g, wq, seq):
    t, d = x2d.shape
    half = lru.shape[1]
    tm = min(512, seq)
    tk = 1024
    nkh = half // tk
    per_batch = seq // tm
    nq = wq.shape[1]
    return pl.pallas_call(
        _outproj_kernel,
        grid=(t // tm, 2 * nkh),
        in_specs=[
            pl.BlockSpec((tm, tk), lambda i, k: (i, jnp.minimum(k, nkh - 1))),
            pl.BlockSpec((tm, tk), lambda i, k: (i, jnp.maximum(k - nkh, 0))),
            pl.BlockSpec((tk, d), lambda i, k: (k, 0)),
            pl.BlockSpec((tm, d), lambda i, k: (i, 0)),
            pl.BlockSpec((None, 6, d), lambda i, k: (i // per_batch, 0, 0)),
            pl.BlockSpec((1, d), lambda i, k: (0, 0)),
            pl.BlockSpec((d, nq), lambda i, k: (0, 0)),
        ],
        out_specs=[
            pl.BlockSpec((tm, d), lambda i, k: (i, 0)),
            pl.BlockSpec((d, tm), lambda i, k: (0, i)),
            pl.BlockSpec((tm, nq), lambda i, k: (i, 0)),
        ],
        out_shape=[
            jax.ShapeDtypeStruct((t, d), F32),
            jax.ShapeDtypeStruct((d, t), BF16),
            jax.ShapeDtypeStruct((t, nq), F32),
        ],
        scratch_shapes=[pltpu.VMEM((tm, d), F32)],
        compiler_params=_cparams(("parallel", "arbitrary")),
        name="outproj",
    )(lru, ssd, wo, x2d, mod3, g.reshape(1, d), wq)


def _sorting_network(n):
    pairs = []

    def merge(lo, hi, r):
        step = r * 2
        if step < hi - lo:
            merge(lo, hi, step)
            merge(lo + r, hi, step)
            pairs.extend((i, i + r) for i in range(lo + r, hi - r, step))
        else:
            pairs.append((lo, lo + r))

    def sort(lo, hi):
        if hi - lo >= 1:
            mid = lo + (hi - lo) // 2
            sort(lo, mid)
            sort(mid + 1, hi)
            merge(lo, hi, 1)

    sort(0, n - 1)
    return pairs


def _top_values(s, count):
    rows, tb = s.shape
    n = rows // SUBLANES
    assert n == count
    a = [s[k * SUBLANES:(k + 1) * SUBLANES, :] for k in range(n)]
    for i, j in _sorting_network(n):
        a[i], a[j] = jnp.maximum(a[i], a[j]), jnp.minimum(a[i], a[j])
    rid = lax.broadcasted_iota(jnp.int32, (count, tb), 0)
    out = jnp.zeros((count, tb), F32)
    for r in range(count):
        m = jnp.max(a[0], axis=0, keepdims=True)
        out = jnp.where(rid == r, m, out)
        hit = a[0] == m
        for k in range(count - 1 - r):
            a[k] = jnp.where(hit, a[k + 1], a[k])
    return out


def _pair_candidates(v1, v2):
    parts = [v1[0:1] + v2]
    parts += [v1[a:a + 1] + v2[0:8] for a in (1, 2, 3)]
    parts += [v1[0:8] + v2[b:b + 1] for b in (0, 1, 2)]
    parts.append(v1[8:16] + v2[0:1])
    return jnp.concatenate(parts, axis=0)


def _kth_and_expsum(cand, count):
    z = jnp.zeros((1, cand.shape[1]), F32)
    m = None
    for _ in range(count):
        m = jnp.max(cand, axis=0, keepdims=True)
        z = z + jnp.exp2(m)
        cand = jnp.where(cand == m, NEG_BIG, cand)
    return m, z


def _route_kernel(q_ref, keys_ref, s1_ref, s2_ref, tau_ref):
    for h in range(PEER_HEADS):
        ss = []
        for half in range(2):
            col = (2 * h + half) * LANES
            qh = q_ref[:, col:col + LANES].astype(BF16)
            s = lax.dot_general(keys_ref[h, half], qh, (((1,), (1,)), ((), ())), preferred_element_type=F32)
            ss.append(s - jnp.max(s, axis=0, keepdims=True))
        s1, s2 = ss
        v1 = _top_values(s1, PEER_TOPK)
        v2 = _top_values(s2, PEER_TOPK)
        _, z = _kth_and_expsum(_pair_candidates(v1, v2), PEER_TOPK)
        lz = jnp.log2(z)
        s1 = s1 - lz
        tau, _ = _kth_and_expsum(_pair_candidates(v1 - lz, v2), PEER_TOPK)
        s1_ref[h] = s1
        s2_ref[h] = s2
        tau_ref[h:h + 1, :] = tau


def _route(q, keys):
    t, nq = q.shape
    tb = min(512, t)
    nk = keys.shape[2]
    return pl.pallas_call(
        _route_kernel,
        grid=(t // tb,),
        in_specs=[
            pl.BlockSpec((tb, nq), lambda i: (i, 0)),
            pl.BlockSpec(keys.shape, lambda i: (0, 0, 0, 0)),
        ],
        out_specs=[
            pl.BlockSpec((PEER_HEADS, nk, tb), lambda i: (0, 0, i)),
            pl.BlockSpec((PEER_HEADS, nk, tb), lambda i: (0, 0, i)),
            pl.BlockSpec((PEER_HEADS, tb), lambda i: (0, i)),
        ],
        out_shape=[
            jax.ShapeDtypeStruct((PEER_HEADS, nk, t), F32),
            jax.ShapeDtypeStruct((PEER_HEADS, nk, t), F32),
            jax.ShapeDtypeStruct((PEER_HEADS, t), F32),
        ],
        compiler_params=_cparams(("parallel",)),
        name="route",
    )(q, keys)


def _vt_kernel(v_ref, o_ref):
    o_ref[...] = v_ref[...].T.astype(BF16)


def _transpose_bf16(v):
    ne, d = v.shape
    eb = 512
    return pl.pallas_call(
        _vt_kernel,
        grid=(ne // eb,),
        in_specs=[pl.BlockSpec((eb, d), lambda e: (e, 0))],
        out_specs=pl.BlockSpec((d, eb), lambda e: (0, e)),
        out_shape=jax.ShapeDtypeStruct((d, ne), BF16),
        compiler_params=_cparams(("parallel",)),
        name="vt",
    )(v)


def _peer_kernel(h2t_ref, u_ref, vt_ref, s1_ref, s2_ref, tau_ref, o_ref, at_scr, gt_scr, *, tc):
    e = pl.program_id(1)
    ts = h2t_ref.shape[1]
    eb = u_ref.shape[0]

    @pl.when(e == 0)
    def _():
        o_ref[...] = jnp.zeros_like(o_ref)

    for c in range(ts // tc):
        cs = slice(c * tc, (c + 1) * tc)
        at_scr[...] = jnp.dot(u_ref[...], h2t_ref[:, cs], preferred_element_type=F32)

        def tile(it, carry):
            rows = pl.ds(pl.multiple_of(it * PEER_KEYS, PEER_KEYS), PEER_KEYS)
            wsum = None
            for h in range(PEER_HEADS):
                tsum = s1_ref[h, pl.ds(it, 1), cs] + s2_ref[h, :, cs]
                term = jnp.where(tsum >= tau_ref[h:h + 1, cs], jnp.exp2(tsum), 0.0)
                wsum = term if wsum is None else wsum + term
            gt_scr[rows, :] = (_gelu(at_scr[rows, :]) * wsum).astype(BF16)
            return carry

        lax.fori_loop(0, eb // PEER_KEYS, tile, 0)
        o_ref[:, cs] += jnp.dot(vt_ref[...], gt_scr[...], preferred_element_type=F32)


def _peer(h2t, u, vt, s1, s2, tau):
    d, t = h2t.shape
    ne = u.shape[0]
    ts = min(1024, t)
    tc = min(512, ts)
    eb = 1024
    return pl.pallas_call(
        functools.partial(_peer_kernel, tc=tc),
        grid=(t // ts, ne // eb),
        in_specs=[
            pl.BlockSpec((d, ts), lambda s, e: (0, s)),
            pl.BlockSpec((eb, d), lambda s, e: (e, 0)),
            pl.BlockSpec((d, eb), lambda s, e: (0, e)),
            pl.BlockSpec((PEER_HEADS, eb // PEER_KEYS, ts), lambda s, e: (0, e, s)),
            pl.BlockSpec((PEER_HEADS, PEER_KEYS, ts), lambda s, e: (0, 0, s)),
            pl.BlockSpec((PEER_HEADS, ts), lambda s, e: (0, s)),
        ],
        out_specs=pl.BlockSpec((d, ts), lambda s, e: (0, s)),
        out_shape=jax.ShapeDtypeStruct((d, t), F32),
        scratch_shapes=[pltpu.VMEM((eb, tc), F32), pltpu.VMEM((eb, tc), BF16)],
        compiler_params=_cparams(("parallel", "arbitrary")),
        name="peer",
    )(h2t, u, vt, s1, s2, tau)


def _final_kernel(x1_ref, pt_ref, mod_ref, g_ref, o_ref):
    x2 = x1_ref[...] + mod_ref[5:6, :] * pt_ref[...].T
    o_ref[...] = x2 * lax.rsqrt(jnp.mean(x2 * x2, axis=-1, keepdims=True) + EPS) * g_ref[...]


def _final(x1, peer_t, mod3, g, seq):
    t, d = x1.shape
    tm = min(512, seq)
    per_batch = seq // tm
    return pl.pallas_call(
        _final_kernel,
        grid=(t // tm,),
        in_specs=[
            pl.BlockSpec((tm, d), lambda i: (i, 0)),
            pl.BlockSpec((d, tm), lambda i: (0, i)),
            pl.BlockSpec((None, 6, d), lambda i: (i // per_batch, 0, 0)),
            pl.BlockSpec((1, d), lambda i: (0, 0)),
        ],
        out_specs=pl.BlockSpec((tm, d), lambda i: (i, 0)),
        out_shape=jax.ShapeDtypeStruct((t, d), F32),
        compiler_params=_cparams(("parallel",)),
        name="final",
    )(x1, peer_t, mod3, g.reshape(1, d))


def _layer(x2d, mod3, batch, seq, norm1_g, w_in, lru_conv_w, lru_conv_b, lru_w_a, lru_b_a, lru_w_i, lru_b_i,
           lru_lambda, ssd_conv_w, ssd_conv_b, ssd_dt_bias, ssd_a_log, ssd_d, ssd_norm_g, w_out, norm2_g,
           peer_w_q, peer_sub_keys, peer_u, peer_v):
    d = x2d.shape[1]
    lru_w = lru_lambda.shape[0]
    ssd_w = ssd_norm_g.shape[0]
    xbc_w = ssd_conv_w.shape[1]
    n_heads = ssd_dt_bias.shape[0]
    o_lru_x = lru_w
    o_ssd_z = 2 * lru_w
    o_ssd_xbc = o_ssd_z + ssd_w
    o_ssd_dt = o_ssd_xbc + xbc_w

    wdt = jnp.pad(w_in[:, o_ssd_dt:o_ssd_dt + n_heads], ((0, 0), (0, LANES - n_heads))).astype(BF16)
    wab = jnp.concatenate([lru_w_a, lru_w_i], axis=-1).astype(BF16)
    pad_h = lambda v: jnp.pad(v, (0, LANES - n_heads)).reshape(1, LANES)
    head_of_channel = jnp.arange(ssd_w, dtype=jnp.int32) // SSD_HEAD_DIM
    sel = (jnp.arange(LANES, dtype=jnp.int32)[:, None] == head_of_channel[None, :]).astype(BF16)
    tri = (jnp.arange(SSD_CHUNK)[:, None] >= jnp.arange(SSD_CHUNK)[None, :]).astype(BF16)
    dsk = jnp.repeat(ssd_d, SSD_HEAD_DIM).reshape(1, ssd_w)

    h, dt_raw = _norm1(x2d, mod3, norm1_g, wdt, seq)
    proj = _inproj(h, w_in[:, :o_ssd_dt].astype(BF16))
    lru = _lru(proj, lru_conv_w, lru_conv_b.reshape(1, lru_w), wab, lru_b_a.reshape(1, lru_w),
               lru_b_i.reshape(1, lru_w), lru_lambda.reshape(1, lru_w), batch, seq, lru_w, o_lru_x)
    ssd = _ssd(proj, dt_raw, ssd_conv_w, ssd_conv_b.reshape(1, xbc_w), pad_h(ssd_dt_bias), pad_h(ssd_a_log),
               dsk, ssd_norm_g.reshape(1, ssd_w), sel, tri, batch, seq, ssd_w, o_ssd_z, o_ssd_xbc)
    x1, h2t, q = _outproj(lru, ssd, w_out.astype(BF16), x2d, mod3, norm2_g, peer_w_q.astype(BF16), seq)
    s1, s2, tau = _route(q, (peer_sub_keys * math.log2(math.e)).astype(BF16))
    peer_t = _peer(h2t, peer_u.astype(BF16), _transpose_bf16(peer_v), s1, s2, tau)
    return x1, peer_t


def kernel(x, c, w_ada, b_ada, norm1_g, w_in, lru_conv_w, lru_conv_b, lru_w_a, lru_b_a, lru_w_i, lru_b_i, lru_lambda, ssd_conv_w, ssd_conv_b, ssd_dt_bias, ssd_a_log, ssd_d, ssd_norm_g, w_out, norm2_g, peer_w_q, peer_sub_keys, peer_u, peer_v, final_norm_g):
    batch, seq, d = x.shape
    depth = w_ada.shape[0]
    assert depth == 1, "the final residual is fused with the last layer; only depth 1 is wired up"
    x2d = x.reshape(batch * seq, d)
    l = 0
    mod3 = _ada(c, w_ada[l], b_ada[l]).reshape(batch, 6, d)
    x1, peer_t = _layer(x2d, mod3, batch, seq, norm1_g[l], w_in[l], lru_conv_w[l], lru_conv_b[l], lru_w_a[l],
                        lru_b_a[l], lru_w_i[l], lru_b_i[l], lru_lambda[l], ssd_conv_w[l], ssd_conv_b[l],
                        ssd_dt_bias[l], ssd_a_log[l], ssd_d[l], ssd_norm_g[l], w_out[l], norm2_g[l],
                        peer_w_q[l], peer_sub_keys[l], peer_u[l], peer_v[l])
    y = _final(x1, peer_t, mod3, final_norm_g, seq)
    return y.reshape(batch, seq, d)
```

```python
import functools
import math

import jax
import jax.numpy as jnp
from jax import lax
from jax.experimental import pallas as pl
from jax.experimental.pallas import tpu as pltpu

F32 = jnp.float32
BF16 = jnp.bfloat16

EPS = 1e-6
CONV_WIDTH = 4
LRU_C = 8.0
LRU_HEAD_DIM = 128
SSD_HEAD_DIM = 64
SSD_GROUPS = 4
SSD_STATE = 128
SSD_CHUNK = 128
PEER_HEADS = 8
PEER_KEYS = 128
PEER_TOPK = 16
LANES = 128
SUBLANES = 8
NEG_BIG = -1e30
VMEM_LIMIT = 56 * 1024 * 1024


def _cparams(sem, flags=None):
    return pltpu.CompilerParams(dimension_semantics=sem, vmem_limit_bytes=VMEM_LIMIT, flags=flags)


def _split3(x):
    hi = x.astype(BF16)
    r1 = x - hi.astype(F32)
    mid = r1.astype(BF16)
    lo = (r1 - mid.astype(F32)).astype(BF16)
    return hi, mid, lo


def _dot_sel_right(x, sel):
    return sum(jnp.dot(p, sel, preferred_element_type=F32) for p in _split3(x))


def _dot_sel_left(sel, x):
    return sum(jnp.dot(sel, p, preferred_element_type=F32) for p in _split3(x))


def _softplus(x):
    return jnp.maximum(x, 0.0) + jnp.log1p(jnp.exp(-jnp.abs(x)))


def _silu(x):
    return x * jax.nn.sigmoid(x)


def _causal_conv(xpad, cw_ref, cb_ref, rows):
    e = xpad[0:SUBLANES + rows, :]
    z = cw_ref[0:1, :] * e
    for k in range(1, CONV_WIDTH):
        z = pltpu.roll(z, 1, 0) + cw_ref[k:k + 1, :] * e
    return z[SUBLANES:, :] + cb_ref[...]


def _gelu(x):
    c = math.sqrt(2.0 / math.pi)
    hx = 0.5 * x
    return hx + hx * jnp.tanh(x * (c + (c * 0.044715) * (x * x)))


def _ada_kernel(c_ref, w_ref, b_ref, o_ref, act_ref):
    nb = c_ref.shape[0]
    tn = w_ref.shape[1]

    @pl.when(pl.program_id(0) == 0)
    def _():
        cv = c_ref[...]
        act_ref[...] = _silu(cv)

    for k in range(tn // LANES):
        sl = slice(k * LANES, (k + 1) * LANES)
        wk = w_ref[:, sl]
        for b in range(nb):
            o_ref[b:b + 1, sl] = jnp.sum(wk * act_ref[b], axis=0, keepdims=True) + b_ref[:, sl]


def _ada(c, w_ada, b_ada):
    nb, d = c.shape
    n = w_ada.shape[1]
    tn = 1024
    cb = jnp.broadcast_to(c[:, :, None], (nb, d, LANES))
    return pl.pallas_call(
        _ada_kernel,
        grid=(n // tn,),
        in_specs=[
            pl.BlockSpec((nb, d, LANES), lambda j: (0, 0, 0)),
            pl.BlockSpec((d, tn), lambda j: (0, j)),
            pl.BlockSpec((1, tn), lambda j: (0, j)),
        ],
        out_specs=pl.BlockSpec((nb, tn), lambda j: (0, j)),
        out_shape=jax.ShapeDtypeStruct((nb, n), F32),
        scratch_shapes=[pltpu.VMEM((nb, d, LANES), F32)],
        compiler_params=_cparams(("arbitrary",)),
        name="ada",
    )(cb, w_ada, b_ada.reshape(1, n))


def _norm1_kernel(x_ref, mod_ref, g_ref, wdt_ref, h_ref, dt_ref):
    x = x_ref[...]
    y = x * lax.rsqrt(jnp.mean(x * x, axis=-1, keepdims=True) + EPS) * g_ref[...]
    h = (y * (1.0 + mod_ref[1:2, :]) + mod_ref[0:1, :]).astype(BF16)
    h_ref[...] = h
    dt_ref[...] = jnp.dot(h, wdt_ref[...], preferred_element_type=F32)


def _norm1(x2d, mod3, g, wdt, seq):
    t, d = x2d.shape
    tm = min(512, seq)
    per_batch = seq // tm
    return pl.pallas_call(
        _norm1_kernel,
        grid=(t // tm,),
        in_specs=[
            pl.BlockSpec((tm, d), lambda i: (i, 0)),
            pl.BlockSpec((None, 6, d), lambda i: (i // per_batch, 0, 0)),
            pl.BlockSpec((1, d), lambda i: (0, 0)),
            pl.BlockSpec((d, LANES), lambda i: (0, 0)),
        ],
        out_specs=[
            pl.BlockSpec((tm, d), lambda i: (i, 0)),
            pl.BlockSpec((tm, LANES), lambda i: (i, 0)),
        ],
        out_shape=[jax.ShapeDtypeStruct((t, d), BF16), jax.ShapeDtypeStruct((t, LANES), F32)],
        compiler_params=_cparams(("parallel",)),
        name="norm1",
    )(x2d, mod3, g.reshape(1, d), wdt)


def _inproj_kernel(h_ref, w_ref, o_ref):
    o_ref[...] = jnp.dot(h_ref[...], w_ref[...], preferred_element_type=F32)


def _inproj(h, w_bf, n_main):
    t, d = h.shape
    tn = 1152
    tm = min(2048, t)
    return pl.pallas_call(
        _inproj_kernel,
        grid=(n_main // tn, t // tm),
        in_specs=[
            pl.BlockSpec((tm, d), lambda j, i: (i, 0)),
            pl.BlockSpec((d, tn), lambda j, i: (0, j)),
        ],
        out_specs=pl.BlockSpec((tm, tn), lambda j, i: (i, j)),
        out_shape=jax.ShapeDtypeStruct((t, n_main), F32),
        compiler_params=_cparams(("parallel", "arbitrary")),
        name="inproj",
    )(h, w_bf)


def _lru_kernel(g_ref, x_ref, cw_ref, cb_ref, wab_ref, ba_ref, bi_ref, lam_ref, o_ref,
                xpad, a_scr, b_scr, hs, hcar):
    tt, c = x_ref.shape
    t = pl.program_id(2)

    @pl.when(t == 0)
    def _():
        xpad[0:SUBLANES, :] = jnp.zeros((SUBLANES, c), F32)
        hcar[...] = jnp.zeros_like(hcar)

    xpad[SUBLANES:SUBLANES + tt, :] = x_ref[...]
    y = _causal_conv(xpad, cw_ref, cb_ref, tt)
    xpad[0:SUBLANES, :] = xpad[tt:tt + SUBLANES, :]

    row = lax.broadcasted_iota(jnp.int32, (tt // SUBLANES, SUBLANES, LRU_HEAD_DIM), 1)
    for j in range(c // LRU_HEAD_DIM):
        sl = slice(j * LRU_HEAD_DIM, (j + 1) * LRU_HEAD_DIM)
        yj = y[:, sl]
        gm = jnp.dot(yj.astype(BF16), wab_ref[j], preferred_element_type=F32)
        r = jax.nn.sigmoid(gm[:, :LRU_HEAD_DIM] + ba_ref[:, sl])
        i = jax.nn.sigmoid(gm[:, LRU_HEAD_DIM:] + bi_ref[:, sl])
        log_a = (-LRU_C * _softplus(-lam_ref[:, sl])) * r
        a = jnp.exp(log_a)
        th = jnp.tanh(log_a)
        b = jnp.sqrt((-2.0 * th) / (1.0 - th)) * (i * yj)
        a = a.reshape(tt // SUBLANES, SUBLANES, LRU_HEAD_DIM)
        b = b.reshape(tt // SUBLANES, SUBLANES, LRU_HEAD_DIM)
        for s in (1, 2, 4):
            a_sh = pltpu.roll(a, s, 1)
            b_sh = pltpu.roll(b, s, 1)
            m = row >= s
            b = jnp.where(m, a * b_sh + b, b)
            a = jnp.where(m, a * a_sh, a)
        a_scr[:, sl] = a.reshape(tt, LRU_HEAD_DIM)
        b_scr[:, sl] = b.reshape(tt, LRU_HEAD_DIM)

    carry = hcar[...]
    for n in range(tt // SUBLANES):
        rs = slice(n * SUBLANES, (n + 1) * SUBLANES)
        hb = b_scr[rs, :] + a_scr[rs, :] * carry
        hs[rs, :] = hb
        carry = hb[SUBLANES - 1:SUBLANES, :]
    hcar[...] = carry
    o_ref[...] = (_gelu(g_ref[...]) * hs[...]).astype(BF16)


def _lru(proj, cw, cb, wab, ba, bi, lam, batch, seq, width, x_col0):
    t = proj.shape[0]
    c = 512
    tt = min(512, seq)
    nt = seq // tt
    ncb = width // c
    hpb = c // LRU_HEAD_DIM
    xoff = x_col0 // c
    return pl.pallas_call(
        _lru_kernel,
        grid=(batch, ncb, nt),
        in_specs=[
            pl.BlockSpec((tt, c), lambda b, j, i: (b * nt + i, j)),
            pl.BlockSpec((tt, c), lambda b, j, i: (b * nt + i, xoff + j)),
            pl.BlockSpec((CONV_WIDTH, c), lambda b, j, i: (0, j)),
            pl.BlockSpec((1, c), lambda b, j, i: (0, j)),
            pl.BlockSpec((hpb, LRU_HEAD_DIM, 2 * LRU_HEAD_DIM), lambda b, j, i: (j, 0, 0)),
            pl.BlockSpec((1, c), lambda b, j, i: (0, j)),
            pl.BlockSpec((1, c), lambda b, j, i: (0, j)),
            pl.BlockSpec((1, c), lambda b, j, i: (0, j)),
        ],
        out_specs=pl.BlockSpec((tt, c), lambda b, j, i: (b * nt + i, j)),
        out_shape=jax.ShapeDtypeStruct((t, width), BF16),
        scratch_shapes=[
            pltpu.VMEM((tt + SUBLANES, c), F32),
            pltpu.VMEM((tt, c), F32),
            pltpu.VMEM((tt, c), F32),
            pltpu.VMEM((tt, c), F32),
            pltpu.VMEM((1, c), F32),
        ],
        compiler_params=_cparams(("parallel", "parallel", "arbitrary")),
        name="lru",
    )(proj, proj, cw, cb, wab, ba, bi, lam)


def _ssd_kernel(z_ref, xbc_ref, dt_ref, cw_ref, cb_ref, dtb_ref, alog_ref, dsk_ref, ng_ref, e_ref, tri_ref,
                o_ref, xpad, st):
    L = SSD_CHUNK
    N = SSD_STATE
    w = z_ref.shape[1]
    gw = w // SSD_GROUPS
    gn = SSD_GROUPS * N

    @pl.when(pl.program_id(1) == 0)
    def _():
        xpad[0:SUBLANES, :] = jnp.zeros((SUBLANES, xpad.shape[1]), F32)
        st[...] = jnp.zeros_like(st)

    xpad[SUBLANES:SUBLANES + L, :] = xbc_ref[...]
    y = _causal_conv(xpad, cw_ref, cb_ref, L)
    xpad[0:SUBLANES, :] = xpad[L:L + SUBLANES, :]
    act = _silu(y)
    xs = act[:, :w]
    bm = act[:, w:w + gn]
    cm = act[:, w + gn:]

    dt = _softplus(dt_ref[...] + dtb_ref[...])
    da = dt * (-jnp.exp(alog_ref[...]))
    a_cs = _dot_sel_left(tri_ref[...], da)
    a_cs_t = a_cs.T
    ea = jnp.exp(a_cs)
    dte = jnp.exp(a_cs[L - 1:L, :] - a_cs)
    sel = e_ref[...]
    dt_f = _dot_sel_right(dt, sel)
    ea_f = _dot_sel_right(ea, sel)
    dte_f = _dot_sel_right(dte, sel)
    xc = xs * dt_f
    xcb = xc.astype(BF16)
    xd = (xc * dte_f).astype(BF16)

    causal = lax.broadcasted_iota(jnp.int32, (L, L), 0) >= lax.broadcasted_iota(jnp.int32, (L, L), 1)
    lane = lax.broadcasted_iota(jnp.int32, (L, LANES), 1)
    low_half = lane < SSD_HEAD_DIM
    heads_per_group = gw // SSD_HEAD_DIM
    for g in range(SSD_GROUPS):
        gs = slice(g * gw, (g + 1) * gw)
        cg = cm[:, g * N:(g + 1) * N].astype(BF16)
        bg = bm[:, g * N:(g + 1) * N]
        cbm = lax.dot_general(cg, bg.astype(BF16), (((1,), (1,)), ((), ())), preferred_element_type=F32)
        bg_t = bg.T.astype(BF16)
        s_prev = st[g]
        y_off = jnp.dot(cg, s_prev.astype(BF16), preferred_element_type=F32) * ea_f[:, gs]
        s_new = jnp.dot(bg_t, xd[:, gs], preferred_element_type=F32)
        st[g] = ea_f[L - 1:L, gs] * s_prev + s_new
        for p in range(heads_per_group // 2):
            h0 = g * heads_per_group + 2 * p
            cs = slice(h0 * SSD_HEAD_DIM, (h0 + 2) * SSD_HEAD_DIM)
            ms = []
            for hh in (h0, h0 + 1):
                seg = a_cs[:, hh:hh + 1] - a_cs_t[hh:hh + 1, :]
                ms.append(cbm * jnp.exp(jnp.where(causal, seg, NEG_BIG)))
            lhs = jnp.concatenate(ms, axis=1).astype(BF16)
            pair = xcb[:, cs]
            zero = jnp.zeros_like(pair)
            rhs = jnp.concatenate([jnp.where(low_half, pair, zero), jnp.where(low_half, zero, pair)], axis=0)
            y_diag = jnp.dot(lhs, rhs, preferred_element_type=F32)
            yp = y_diag + y_off[:, 2 * p * SSD_HEAD_DIM:(2 * p + 2) * SSD_HEAD_DIM] + dsk_ref[:, cs] * xs[:, cs]
            xpad[SUBLANES:SUBLANES + L, cs] = yp

    yz = xpad[SUBLANES:SUBLANES + L, 0:w] * _silu(z_ref[...])
    for g in range(SSD_GROUPS):
        gs = slice(g * gw, (g + 1) * gw)
        v = yz[:, gs]
        v = v * lax.rsqrt(jnp.mean(v * v, axis=-1, keepdims=True) + EPS)
        o_ref[:, gs] = (v * ng_ref[:, gs]).astype(BF16)


def _ssd(proj, dt_raw, cw, cb, dtb, alog, dsk, ng, sel, tri, batch, seq, width, z_col0, xbc_col0):
    t = proj.shape[0]
    xw = cw.shape[1]
    L = SSD_CHUNK
    nc = seq // L
    zoff = z_col0 // width
    xoff = xbc_col0 // xw
    return pl.pallas_call(
        _ssd_kernel,
        grid=(batch, nc),
        in_specs=[
            pl.BlockSpec((L, width), lambda b, i: (b * nc + i, zoff)),
            pl.BlockSpec((L, xw), lambda b, i: (b * nc + i, xoff)),
            pl.BlockSpec((L, LANES), lambda b, i: (b * nc + i, 0)),
            pl.BlockSpec((CONV_WIDTH, xw), lambda b, i: (0, 0)),
            pl.BlockSpec((1, xw), lambda b, i: (0, 0)),
            pl.BlockSpec((1, LANES), lambda b, i: (0, 0)),
            pl.BlockSpec((1, LANES), lambda b, i: (0, 0)),
            pl.BlockSpec((1, width), lambda b, i: (0, 0)),
            pl.BlockSpec((1, width), lambda b, i: (0, 0)),
            pl.BlockSpec((LANES, width), lambda b, i: (0, 0)),
            pl.BlockSpec((L, L), lambda b, i: (0, 0)),
        ],
        out_specs=pl.BlockSpec((L, width), lambda b, i: (b * nc + i, 0)),
        out_shape=jax.ShapeDtypeStruct((t, width), BF16),
        scratch_shapes=[
            pltpu.VMEM((L + SUBLANES, xw), F32),
            pltpu.VMEM((SSD_GROUPS, SSD_STATE, width // SSD_GROUPS), F32),
        ],
        compiler_params=_cparams(("parallel", "arbitrary")),
        name="ssd",
    )(proj, proj, dt_raw, cw, cb, dtb, alog, dsk, ng, sel, tri)


def _outproj_kernel(lru_ref, ssd_ref, wo_ref, x_ref, mod_ref, g_ref, wq_ref, x1_ref, h2t_ref, q_ref, acc):
    k = pl.program_id(1)
    nk = pl.num_programs(1)

    @pl.when(k == 0)
    def _():
        acc[...] = jnp.zeros_like(acc)

    @pl.when(k < nk // 2)
    def _():
        acc[...] += jnp.dot(lru_ref[...], wo_ref[...], preferred_element_type=F32)

    @pl.when(k >= nk // 2)
    def _():
        acc[...] += jnp.dot(ssd_ref[...], wo_ref[...], preferred_element_type=F32)

    @pl.when(k == nk - 1)
    def _():
        x1 = x_ref[...] + mod_ref[2:3, :] * acc[...]
        x1_ref[...] = x1
        y = x1 * lax.rsqrt(jnp.mean(x1 * x1, axis=-1, keepdims=True) + EPS) * g_ref[...]
        h2 = y * (1.0 + mod_ref[4:5, :]) + mod_ref[3:4, :]
        h2t_ref[...] = h2.T.astype(BF16)
        q_ref[...] = jnp.dot(h2.astype(BF16), wq_ref[...], preferred_element_type=F32)


def _outproj(lru, ssd, wo, x2d, mod3, g, wq, seq):
    t, d = x2d.shape
    half = lru.shape[1]
    tm = min(512, seq)
    tk = 1024
    nkh = half // tk
    per_batch = seq // tm
    nq = wq.shape[1]
    return pl.pallas_call(
        _outproj_kernel,
        grid=(t // tm, 2 * nkh),
        in_specs=[
            pl.BlockSpec((tm, tk), lambda i, k: (i, jnp.minimum(k, nkh - 1))),
            pl.BlockSpec((tm, tk), lambda i, k: (i, jnp.maximum(k - nkh, 0))),
            pl.BlockSpec((tk, d), lambda i, k: (k, 0)),
            pl.BlockSpec((tm, d), lambda i, k: (i, 0)),
            pl.BlockSpec((None, 6, d), lambda i, k: (i // per_batch, 0, 0)),
            pl.BlockSpec((1, d), lambda i, k: (0, 0)),
            pl.BlockSpec((d, nq), lambda i, k: (0, 0)),
        ],
        out_specs=[
            pl.BlockSpec((tm, d), lambda i, k: (i, 0)),
            pl.BlockSpec((d, tm), lambda i, k: (0, i)),
            pl.BlockSpec((tm, nq), lambda i, k: (i, 0)),
        ],
        out_shape=[
            jax.ShapeDtypeStruct((t, d), F32),
            jax.ShapeDtypeStruct((d, t), BF16),
            jax.ShapeDtypeStruct((t, nq), F32),
        ],
        scratch_shapes=[pltpu.VMEM((tm, d), F32)],
        compiler_params=_cparams(("parallel", "arbitrary")),
        name="outproj",
    )(lru, ssd, wo, x2d, mod3, g.reshape(1, d), wq)


def _sorting_network(n):
    pairs = []

    def merge(lo, hi, r):
        step = r * 2
        if step < hi - lo:
            merge(lo, hi, step)
            merge(lo + r, hi, step)
            pairs.extend((i, i + r) for i in range(lo + r, hi - r, step))
        else:
            pairs.append((lo, lo + r))

    def sort(lo, hi):
        if hi - lo >= 1:
            mid = lo + (hi - lo) // 2
            sort(lo, mid)
            sort(mid + 1, hi)
            merge(lo, hi, 1)

    sort(0, n - 1)
    return pairs


def _top_values(s, count):
    rows, tb = s.shape
    n = rows // SUBLANES
    assert n == count
    a = [s[k * SUBLANES:(k + 1) * SUBLANES, :] for k in range(n)]
    for i, j in _sorting_network(n):
        a[i], a[j] = jnp.maximum(a[i], a[j]), jnp.minimum(a[i], a[j])
    rid = lax.broadcasted_iota(jnp.int32, (count, tb), 0)
    out = jnp.zeros((count, tb), F32)
    for r in range(count):
        m = jnp.max(a[0], axis=0, keepdims=True)
        out = jnp.where(rid == r, m, out)
        hit = a[0] == m
        for k in range(count - 1 - r):
            a[k] = jnp.where(hit, a[k + 1], a[k])
    return out


def _pair_candidates(v1, v2):
    parts = [v1[0:1] + v2]
    parts += [v1[a:a + 1] + v2[0:8] for a in (1, 2, 3)]
    parts += [v1[0:8] + v2[b:b + 1] for b in (0, 1, 2)]
    parts.append(v1[8:16] + v2[0:1])
    return jnp.concatenate(parts, axis=0)


def _kth_and_expsum(cand, count):
    z = jnp.zeros((1, cand.shape[1]), F32)
    m = None
    for _ in range(count):
        m = jnp.max(cand, axis=0, keepdims=True)
        z = z + jnp.exp2(m)
        cand = jnp.where(cand == m, NEG_BIG, cand)
    return m, z


def _route_kernel(q_ref, keys_ref, s1_ref, s2_ref, tau_ref):
    for h in range(PEER_HEADS):
        ss = []
        for half in range(2):
            col = (2 * h + half) * LANES
            qh = q_ref[:, col:col + LANES].astype(BF16)
            s = lax.dot_general(keys_ref[h, half], qh, (((1,), (1,)), ((), ())), preferred_element_type=F32)
            ss.append(s - jnp.max(s, axis=0, keepdims=True))
        s1, s2 = ss
        v1 = _top_values(s1, PEER_TOPK)
        v2 = _top_values(s2, PEER_TOPK)
        _, z = _kth_and_expsum(_pair_candidates(v1, v2), PEER_TOPK)
        lz = jnp.log2(z)
        s1 = s1 - lz
        tau, _ = _kth_and_expsum(_pair_candidates(v1 - lz, v2), PEER_TOPK)
        s1_ref[h] = s1
        s2_ref[h] = s2
        tau_ref[h:h + 1, :] = tau


def _route(q, keys):
    t, nq = q.shape
    tb = min(512, t)
    nk = keys.shape[2]
    return pl.pallas_call(
        _route_kernel,
        grid=(t // tb,),
        in_specs=[
            pl.BlockSpec((tb, nq), lambda i: (i, 0)),
            pl.BlockSpec(keys.shape, lambda i: (0, 0, 0, 0)),
        ],
        out_specs=[
            pl.BlockSpec((PEER_HEADS, nk, tb), lambda i: (0, 0, i)),
            pl.BlockSpec((PEER_HEADS, nk, tb), lambda i: (0, 0, i)),
            pl.BlockSpec((PEER_HEADS, tb), lambda i: (0, i)),
        ],
        out_shape=[
            jax.ShapeDtypeStruct((PEER_HEADS, nk, t), F32),
            jax.ShapeDtypeStruct((PEER_HEADS, nk, t), F32),
            jax.ShapeDtypeStruct((PEER_HEADS, t), F32),
        ],
        compiler_params=_cparams(("parallel",)),
        name="route",
    )(q, keys)


def _vt_kernel(v_ref, o_ref):
    o_ref[...] = v_ref[...].T.astype(BF16)


def _transpose_bf16(v):
    ne, d = v.shape
    eb = 512
    return pl.pallas_call(
        _vt_kernel,
        grid=(ne // eb,),
        in_specs=[pl.BlockSpec((eb, d), lambda e: (e, 0))],
        out_specs=pl.BlockSpec((d, eb), lambda e: (0, e)),
        out_shape=jax.ShapeDtypeStruct((d, ne), BF16),
        compiler_params=_cparams(("parallel",)),
        name="vt",
    )(v)


def _peer_kernel(h2t_ref, u_ref, vt_ref, s1_ref, s2_ref, tau_ref, o_ref, at_scr, gt_scr, *, tc):
    e = pl.program_id(1)
    ts = h2t_ref.shape[1]
    eb = u_ref.shape[0]

    @pl.when(e == 0)
    def _():
        o_ref[...] = jnp.zeros_like(o_ref)

    for c in range(ts // tc):
        cs = slice(c * tc, (c + 1) * tc)
        at_scr[...] = jnp.dot(u_ref[...], h2t_ref[:, cs], preferred_element_type=F32)

        def tile(it, carry):
            rows = pl.ds(pl.multiple_of(it * PEER_KEYS, PEER_KEYS), PEER_KEYS)
            wsum = None
            for h in range(PEER_HEADS):
                tsum = s1_ref[h, pl.ds(it, 1), cs] + s2_ref[h, :, cs]
                term = jnp.where(tsum >= tau_ref[h:h + 1, cs], jnp.exp2(tsum), 0.0)
                wsum = term if wsum is None else wsum + term
            gt_scr[rows, :] = wsum
            return carry

        lax.fori_loop(0, eb // PEER_KEYS, tile, 0)
        g = (_gelu(at_scr[...]) * gt_scr[...]).astype(BF16)
        o_ref[:, cs] += jnp.dot(vt_ref[...], g, preferred_element_type=F32)


def _peer(h2t, u, vt, s1, s2, tau):
    d, t = h2t.shape
    ne = u.shape[0]
    ts = min(512, t)
    tc = min(512, ts)
    eb = 1024
    return pl.pallas_call(
        functools.partial(_peer_kernel, tc=tc),
        grid=(t // ts, ne // eb),
        in_specs=[
            pl.BlockSpec((d, ts), lambda s, e: (0, s)),
            pl.BlockSpec((eb, d), lambda s, e: (e, 0)),
            pl.BlockSpec((d, eb), lambda s, e: (0, e)),
            pl.BlockSpec((PEER_HEADS, eb // PEER_KEYS, ts), lambda s, e: (0, e, s)),
            pl.BlockSpec((PEER_HEADS, PEER_KEYS, ts), lambda s, e: (0, 0, s)),
            pl.BlockSpec((PEER_HEADS, ts), lambda s, e: (0, s)),
        ],
        out_specs=pl.BlockSpec((d, ts), lambda s, e: (0, s)),
        out_shape=jax.ShapeDtypeStruct((d, t), F32),
        scratch_shapes=[pltpu.VMEM((eb, tc), F32), pltpu.VMEM((eb, tc), F32)],
        compiler_params=_cparams(("parallel", "arbitrary")),
        name="peer",
    )(h2t, u, vt, s1, s2, tau)


def _final_kernel(x1_ref, pt_ref, mod_ref, g_ref, o_ref):
    x2 = x1_ref[...] + mod_ref[5:6, :] * pt_ref[...].T
    o_ref[...] = x2 * lax.rsqrt(jnp.mean(x2 * x2, axis=-1, keepdims=True) + EPS) * g_ref[...]


def _final(x1, peer_t, mod3, g, seq):
    t, d = x1.shape
    tm = min(512, seq)
    per_batch = seq // tm
    return pl.pallas_call(
        _final_kernel,
        grid=(t // tm,),
        in_specs=[
            pl.BlockSpec((tm, d), lambda i: (i, 0)),
            pl.BlockSpec((d, tm), lambda i: (0, i)),
            pl.BlockSpec((None, 6, d), lambda i: (i // per_batch, 0, 0)),
            pl.BlockSpec((1, d), lambda i: (0, 0)),
        ],
        out_specs=pl.BlockSpec((tm, d), lambda i: (i, 0)),
        out_shape=jax.ShapeDtypeStruct((t, d), F32),
        compiler_params=_cparams(("parallel",)),
        name="final",
    )(x1, peer_t, mod3, g.reshape(1, d))


def _layer(x2d, mod3, batch, seq, norm1_g, w_in, lru_conv_w, lru_conv_b, lru_w_a, lru_b_a, lru_w_i, lru_b_i,
           lru_lambda, ssd_conv_w, ssd_conv_b, ssd_dt_bias, ssd_a_log, ssd_d, ssd_norm_g, w_out, norm2_g,
           peer_w_q, peer_sub_keys, peer_u, peer_v):
    d = x2d.shape[1]
    lru_w = lru_lambda.shape[0]
    ssd_w = ssd_norm_g.shape[0]
    xbc_w = ssd_conv_w.shape[1]
    n_heads = ssd_dt_bias.shape[0]
    o_lru_x = lru_w
    o_ssd_z = 2 * lru_w
    o_ssd_xbc = o_ssd_z + ssd_w
    o_ssd_dt = o_ssd_xbc + xbc_w

    w_bf = w_in.astype(BF16)
    wdt = jnp.pad(w_bf[:, o_ssd_dt:o_ssd_dt + n_heads], ((0, 0), (0, LANES - n_heads)))
    wab = jnp.concatenate([lru_w_a, lru_w_i], axis=-1).astype(BF16)
    pad_h = lambda v: jnp.pad(v, (0, LANES - n_heads)).reshape(1, LANES)
    head_of_channel = jnp.arange(ssd_w, dtype=jnp.int32) // SSD_HEAD_DIM
    sel = (jnp.arange(LANES, dtype=jnp.int32)[:, None] == head_of_channel[None, :]).astype(BF16)
    tri = (jnp.arange(SSD_CHUNK)[:, None] >= jnp.arange(SSD_CHUNK)[None, :]).astype(BF16)
    dsk = jnp.repeat(ssd_d, SSD_HEAD_DIM).reshape(1, ssd_w)

    h, dt_raw = _norm1(x2d, mod3, norm1_g, wdt, seq)
    proj = _inproj(h, w_bf, o_ssd_dt)
    lru = _lru(proj, lru_conv_w, lru_conv_b.reshape(1, lru_w), wab, lru_b_a.reshape(1, lru_w),
               lru_b_i.reshape(1, lru_w), lru_lambda.reshape(1, lru_w), batch, seq, lru_w, o_lru_x)
    ssd = _ssd(proj, dt_raw, ssd_conv_w, ssd_conv_b.reshape(1, xbc_w), pad_h(ssd_dt_bias), pad_h(ssd_a_log),
               dsk, ssd_norm_g.reshape(1, ssd_w), sel, tri, batch, seq, ssd_w, o_ssd_z, o_ssd_xbc)
    x1, h2t, q = _outproj(lru, ssd, w_out.astype(BF16), x2d, mod3, norm2_g, peer_w_q.astype(BF16), seq)
    s1, s2, tau = _route(q, (peer_sub_keys * math.log2(math.e)).astype(BF16))
    peer_t = _peer(h2t, peer_u.astype(BF16), _transpose_bf16(peer_v), s1, s2, tau)
    return x1, peer_t


def kernel(x, c, w_ada, b_ada, norm1_g, w_in, lru_conv_w, lru_conv_b, lru_w_a, lru_b_a, lru_w_i, lru_b_i, lru_lambda, ssd_conv_w, ssd_conv_b, ssd_dt_bias, ssd_a_log, ssd_d, ssd_norm_g, w_out, norm2_g, peer_w_q, peer_sub_keys, peer_u, peer_v, final_norm_g):
    batch, seq, d = x.shape
    depth = w_ada.shape[0]
    assert depth == 1, "the final residual is fused with the last layer; only depth 1 is wired up"
    x2d = x.reshape(batch * seq, d)
    l = 0
    mod3 = _ada(c, w_ada[l], b_ada[l]).reshape(batch, 6, d)
    x1, peer_t = _layer(x2d, mod3, batch, seq, norm1_g[l], w_in[l], lru_conv_w[l], lru_conv_b[l], lru_w_a[l],
                        lru_b_a[l], lru_w_i[l], lru_b_i[l], lru_lambda[l], ssd_conv_w[l], ssd_conv_b[l],
                        ssd_dt_bias[l], ssd_a_log[l], ssd_d[l], ssd_norm_g[l], w_out[l], norm2_g[l],
                        peer_w_q[l], peer_sub_keys[l], peer_u[l], peer_v[l])
    y = _final(x1, peer_t, mod3, final_norm_g, seq)
    return y.reshape(batch, seq, d)
```

```python
import functools
import math

import jax
import jax.numpy as jnp
from jax import lax
from jax.experimental import pallas as pl
from jax.experimental.pallas import tpu as pltpu

F32 = jnp.float32
BF16 = jnp.bfloat16

EPS = 1e-6
CONV_WIDTH = 4
LRU_C = 8.0
LRU_HEAD_DIM = 128
SSD_HEAD_DIM = 64
SSD_GROUPS = 4
SSD_STATE = 128
SSD_CHUNK = 128
PEER_HEADS = 8
PEER_KEYS = 128
PEER_TOPK = 16
LANES = 128
SUBLANES = 8
NEG_BIG = -1e30
VMEM_LIMIT = 56 * 1024 * 1024


def _cparams(sem, flags=None):
    return pltpu.CompilerParams(dimension_semantics=sem, vmem_limit_bytes=VMEM_LIMIT, flags=flags)


def _split3(x):
    hi = x.astype(BF16)
    r1 = x - hi.astype(F32)
    mid = r1.astype(BF16)
    lo = (r1 - mid.astype(F32)).astype(BF16)
    return hi, mid, lo


def _dot_sel_right(x, sel):
    return sum(jnp.dot(p, sel, preferred_element_type=F32) for p in _split3(x))


def _dot_sel_left(sel, x):
    return sum(jnp.dot(sel, p, preferred_element_type=F32) for p in _split3(x))


def _softplus(x):
    return jnp.maximum(x, 0.0) + jnp.log1p(jnp.exp(-jnp.abs(x)))


def _silu(x):
    return x * jax.nn.sigmoid(x)


def _causal_conv(xpad, cw_ref, cb_ref, rows):
    e = xpad[0:SUBLANES + rows, :]
    z = cw_ref[0:1, :] * e
    for k in range(1, CONV_WIDTH):
        z = pltpu.roll(z, 1, 0) + cw_ref[k:k + 1, :] * e
    return z[SUBLANES:, :] + cb_ref[...]


def _gelu(x):
    c = math.sqrt(2.0 / math.pi)
    hx = 0.5 * x
    return hx + hx * jnp.tanh(x * (c + (c * 0.044715) * (x * x)))


def _ada_kernel(c_ref, w_ref, b_ref, o_ref, act_ref):
    nb = c_ref.shape[0]
    tn = w_ref.shape[1]

    @pl.when(pl.program_id(0) == 0)
    def _():
        cv = c_ref[...]
        act_ref[...] = _silu(cv)

    for k in range(tn // LANES):
        sl = slice(k * LANES, (k + 1) * LANES)
        wk = w_ref[:, sl]
        for b in range(nb):
            o_ref[b:b + 1, sl] = jnp.sum(wk * act_ref[b], axis=0, keepdims=True) + b_ref[:, sl]


def _ada(c, w_ada, b_ada):
    nb, d = c.shape
    n = w_ada.shape[1]
    tn = 1024
    cb = jnp.broadcast_to(c[:, :, None], (nb, d, LANES))
    return pl.pallas_call(
        _ada_kernel,
        grid=(n // tn,),
        in_specs=[
            pl.BlockSpec((nb, d, LANES), lambda j: (0, 0, 0)),
            pl.BlockSpec((d, tn), lambda j: (0, j)),
            pl.BlockSpec((1, tn), lambda j: (0, j)),
        ],
        out_specs=pl.BlockSpec((nb, tn), lambda j: (0, j)),
        out_shape=jax.ShapeDtypeStruct((nb, n), F32),
        scratch_shapes=[pltpu.VMEM((nb, d, LANES), F32)],
        compiler_params=_cparams(("arbitrary",)),
        name="ada",
    )(cb, w_ada, b_ada.reshape(1, n))


def _norm1_kernel(x_ref, mod_ref, g_ref, wdt_ref, h_ref, dt_ref):
    x = x_ref[...]
    y = x * lax.rsqrt(jnp.mean(x * x, axis=-1, keepdims=True) + EPS) * g_ref[...]
    h = (y * (1.0 + mod_ref[1:2, :]) + mod_ref[0:1, :]).astype(BF16)
    h_ref[...] = h
    dt_ref[...] = jnp.dot(h, wdt_ref[...], preferred_element_type=F32)


def _norm1(x2d, mod3, g, wdt, seq):
    t, d = x2d.shape
    tm = min(512, seq)
    per_batch = seq // tm
    return pl.pallas_call(
        _norm1_kernel,
        grid=(t // tm,),
        in_specs=[
            pl.BlockSpec((tm, d), lambda i: (i, 0)),
            pl.BlockSpec((None, 6, d), lambda i: (i // per_batch, 0, 0)),
            pl.BlockSpec((1, d), lambda i: (0, 0)),
            pl.BlockSpec((d, LANES), lambda i: (0, 0)),
        ],
        out_specs=[
            pl.BlockSpec((tm, d), lambda i: (i, 0)),
            pl.BlockSpec((tm, LANES), lambda i: (i, 0)),
        ],
        out_shape=[jax.ShapeDtypeStruct((t, d), BF16), jax.ShapeDtypeStruct((t, LANES), F32)],
        compiler_params=_cparams(("parallel",)),
        name="norm1",
    )(x2d, mod3, g.reshape(1, d), wdt)


def _inproj_kernel(h_ref, w_ref, o_ref):
    o_ref[...] = jnp.dot(h_ref[...], w_ref[...], preferred_element_type=F32)


def _inproj(h, w_bf, n_main):
    t, d = h.shape
    tn = 1152
    tm = min(2048, t)
    return pl.pallas_call(
        _inproj_kernel,
        grid=(n_main // tn, t // tm),
        in_specs=[
            pl.BlockSpec((tm, d), lambda j, i: (i, 0)),
            pl.BlockSpec((d, tn), lambda j, i: (0, j)),
        ],
        out_specs=pl.BlockSpec((tm, tn), lambda j, i: (i, j)),
        out_shape=jax.ShapeDtypeStruct((t, n_main), F32),
        compiler_params=_cparams(("parallel", "arbitrary")),
        name="inproj",
    )(h, w_bf)


def _lru_kernel(g_ref, x_ref, cw_ref, cb_ref, wab_ref, ba_ref, bi_ref, lam_ref, o_ref,
                xpad, a_scr, b_scr, hs, hcar):
    tt, c = x_ref.shape
    t = pl.program_id(2)

    @pl.when(t == 0)
    def _():
        xpad[0:SUBLANES, :] = jnp.zeros((SUBLANES, c), F32)
        hcar[...] = jnp.zeros_like(hcar)

    xpad[SUBLANES:SUBLANES + tt, :] = x_ref[...]
    y = _causal_conv(xpad, cw_ref, cb_ref, tt)
    xpad[0:SUBLANES, :] = xpad[tt:tt + SUBLANES, :]

    row = lax.broadcasted_iota(jnp.int32, (tt // SUBLANES, SUBLANES, LRU_HEAD_DIM), 1)
    for j in range(c // LRU_HEAD_DIM):
        sl = slice(j * LRU_HEAD_DIM, (j + 1) * LRU_HEAD_DIM)
        yj = y[:, sl]
        gm = jnp.dot(yj.astype(BF16), wab_ref[j], preferred_element_type=F32)
        r = jax.nn.sigmoid(gm[:, :LRU_HEAD_DIM] + ba_ref[:, sl])
        i = jax.nn.sigmoid(gm[:, LRU_HEAD_DIM:] + bi_ref[:, sl])
        log_a = (-LRU_C * _softplus(-lam_ref[:, sl])) * r
        a = jnp.exp(log_a)
        th = jnp.tanh(log_a)
        b = jnp.sqrt((-2.0 * th) / (1.0 - th)) * (i * yj)
        a = a.reshape(tt // SUBLANES, SUBLANES, LRU_HEAD_DIM)
        b = b.reshape(tt // SUBLANES, SUBLANES, LRU_HEAD_DIM)
        for s in (1, 2, 4):
            a_sh = pltpu.roll(a, s, 1)
            b_sh = pltpu.roll(b, s, 1)
            m = row >= s
            b = jnp.where(m, a * b_sh + b, b)
            a = jnp.where(m, a * a_sh, a)
        a_scr[:, sl] = a.reshape(tt, LRU_HEAD_DIM)
        b_scr[:, sl] = b.reshape(tt, LRU_HEAD_DIM)

    carry = hcar[...]
    for n in range(tt // SUBLANES):
        rs = slice(n * SUBLANES, (n + 1) * SUBLANES)
        hb = b_scr[rs, :] + a_scr[rs, :] * carry
        hs[rs, :] = hb
        carry = hb[SUBLANES - 1:SUBLANES, :]
    hcar[...] = carry
    o_ref[...] = (_gelu(g_ref[...]) * hs[...]).astype(BF16)


def _lru(proj, cw, cb, wab, ba, bi, lam, batch, seq, width, x_col0):
    t = proj.shape[0]
    c = 512
    tt = min(512, seq)
    nt = seq // tt
    ncb = width // c
    hpb = c // LRU_HEAD_DIM
    xoff = x_col0 // c
    return pl.pallas_call(
        _lru_kernel,
        grid=(batch, ncb, nt),
        in_specs=[
            pl.BlockSpec((tt, c), lambda b, j, i: (b * nt + i, j)),
            pl.BlockSpec((tt, c), lambda b, j, i: (b * nt + i, xoff + j)),
            pl.BlockSpec((CONV_WIDTH, c), lambda b, j, i: (0, j)),
            pl.BlockSpec((1, c), lambda b, j, i: (0, j)),
            pl.BlockSpec((hpb, LRU_HEAD_DIM, 2 * LRU_HEAD_DIM), lambda b, j, i: (j, 0, 0)),
            pl.BlockSpec((1, c), lambda b, j, i: (0, j)),
            pl.BlockSpec((1, c), lambda b, j, i: (0, j)),
            pl.BlockSpec((1, c), lambda b, j, i: (0, j)),
        ],
        out_specs=pl.BlockSpec((tt, c), lambda b, j, i: (b * nt + i, j)),
        out_shape=jax.ShapeDtypeStruct((t, width), BF16),
        scratch_shapes=[
            pltpu.VMEM((tt + SUBLANES, c), F32),
            pltpu.VMEM((tt, c), F32),
            pltpu.VMEM((tt, c), F32),
            pltpu.VMEM((tt, c), F32),
            pltpu.VMEM((1, c), F32),
        ],
        compiler_params=_cparams(("parallel", "parallel", "arbitrary")),
        name="lru",
    )(proj, proj, cw, cb, wab, ba, bi, lam)


def _ssd_kernel(z_ref, xbc_ref, dt_ref, cw_ref, cb_ref, dtb_ref, alog_ref, dsk_ref, ng_ref, e_ref, tri_ref,
                o_ref, xpad, st):
    L = SSD_CHUNK
    N = SSD_STATE
    w = z_ref.shape[1]
    gw = w // SSD_GROUPS
    gn = SSD_GROUPS * N

    @pl.when(pl.program_id(1) == 0)
    def _():
        xpad[0:SUBLANES, :] = jnp.zeros((SUBLANES, xpad.shape[1]), F32)
        st[...] = jnp.zeros_like(st)

    xpad[SUBLANES:SUBLANES + L, :] = xbc_ref[...]
    y = _causal_conv(xpad, cw_ref, cb_ref, L)
    xpad[0:SUBLANES, :] = xpad[L:L + SUBLANES, :]
    act = _silu(y)
    xs = act[:, :w]
    bm = act[:, w:w + gn]
    cm = act[:, w + gn:]

    dt = _softplus(dt_ref[...] + dtb_ref[...])
    da = dt * (-jnp.exp(alog_ref[...]))
    a_cs = _dot_sel_left(tri_ref[...], da)
    a_cs_t = a_cs.T
    ea = jnp.exp(a_cs)
    dte = jnp.exp(a_cs[L - 1:L, :] - a_cs)
    sel = e_ref[...]
    dt_f = _dot_sel_right(dt, sel)
    ea_f = _dot_sel_right(ea, sel)
    dte_f = _dot_sel_right(dte, sel)
    xc = xs * dt_f
    xcb = xc.astype(BF16)
    xd = (xc * dte_f).astype(BF16)

    causal = lax.broadcasted_iota(jnp.int32, (L, L), 0) >= lax.broadcasted_iota(jnp.int32, (L, L), 1)
    lane = lax.broadcasted_iota(jnp.int32, (L, LANES), 1)
    low_half = lane < SSD_HEAD_DIM
    heads_per_group = gw // SSD_HEAD_DIM
    for g in range(SSD_GROUPS):
        gs = slice(g * gw, (g + 1) * gw)
        cg = cm[:, g * N:(g + 1) * N].astype(BF16)
        bg = bm[:, g * N:(g + 1) * N]
        cbm = lax.dot_general(cg, bg.astype(BF16), (((1,), (1,)), ((), ())), preferred_element_type=F32)
        bg_t = bg.T.astype(BF16)
        s_prev = st[g]
        y_off = jnp.dot(cg, s_prev.astype(BF16), preferred_element_type=F32) * ea_f[:, gs]
        s_new = jnp.dot(bg_t, xd[:, gs], preferred_element_type=F32)
        st[g] = ea_f[L - 1:L, gs] * s_prev + s_new
        for p in range(heads_per_group // 2):
            h0 = g * heads_per_group + 2 * p
            cs = slice(h0 * SSD_HEAD_DIM, (h0 + 2) * SSD_HEAD_DIM)
            ms = []
            for hh in (h0, h0 + 1):
                seg = a_cs[:, hh:hh + 1] - a_cs_t[hh:hh + 1, :]
                ms.append(cbm * jnp.exp(jnp.where(causal, seg, NEG_BIG)))
            lhs = jnp.concatenate(ms, axis=1).astype(BF16)
            pair = xcb[:, cs]
            zero = jnp.zeros_like(pair)
            rhs = jnp.concatenate([jnp.where(low_half, pair, zero), jnp.where(low_half, zero, pair)], axis=0)
            y_diag = jnp.dot(lhs, rhs, preferred_element_type=F32)
            yp = y_diag + y_off[:, 2 * p * SSD_HEAD_DIM:(2 * p + 2) * SSD_HEAD_DIM] + dsk_ref[:, cs] * xs[:, cs]
            xpad[SUBLANES:SUBLANES + L, cs] = yp

    yz = xpad[SUBLANES:SUBLANES + L, 0:w] * _silu(z_ref[...])
    for g in range(SSD_GROUPS):
        gs = slice(g * gw, (g + 1) * gw)
        v = yz[:, gs]
        v = v * lax.rsqrt(jnp.mean(v * v, axis=-1, keepdims=True) + EPS)
        o_ref[:, gs] = (v * ng_ref[:, gs]).astype(BF16)


def _ssd(proj, dt_raw, cw, cb, dtb, alog, dsk, ng, sel, tri, batch, seq, width, z_col0, xbc_col0):
    t = proj.shape[0]
    xw = cw.shape[1]
    L = SSD_CHUNK
    nc = seq // L
    zoff = z_col0 // width
    xoff = xbc_col0 // xw
    return pl.pallas_call(
        _ssd_kernel,
        grid=(batch, nc),
        in_specs=[
            pl.BlockSpec((L, width), lambda b, i: (b * nc + i, zoff)),
            pl.BlockSpec((L, xw), lambda b, i: (b * nc + i, xoff)),
            pl.BlockSpec((L, LANES), lambda b, i: (b * nc + i, 0)),
            pl.BlockSpec((CONV_WIDTH, xw), lambda b, i: (0, 0)),
            pl.BlockSpec((1, xw), lambda b, i: (0, 0)),
            pl.BlockSpec((1, LANES), lambda b, i: (0, 0)),
            pl.BlockSpec((1, LANES), lambda b, i: (0, 0)),
            pl.BlockSpec((1, width), lambda b, i: (0, 0)),
            pl.BlockSpec((1, width), lambda b, i: (0, 0)),
            pl.BlockSpec((LANES, width), lambda b, i: (0, 0)),
            pl.BlockSpec((L, L), lambda b, i: (0, 0)),
        ],
        out_specs=pl.BlockSpec((L, width), lambda b, i: (b * nc + i, 0)),
        out_shape=jax.ShapeDtypeStruct((t, width), BF16),
        scratch_shapes=[
            pltpu.VMEM((L + SUBLANES, xw), F32),
            pltpu.VMEM((SSD_GROUPS, SSD_STATE, width // SSD_GROUPS), F32),
        ],
        compiler_params=_cparams(("parallel", "arbitrary")),
        name="ssd",
    )(proj, proj, dt_raw, cw, cb, dtb, alog, dsk, ng, sel, tri)


def _outproj_kernel(lru_ref, ssd_ref, wo_ref, x_ref, mod_ref, g_ref, wq_ref, x1_ref, h2t_ref, q_ref, acc):
    k = pl.program_id(1)
    nk = pl.num_programs(1)

    @pl.when(k == 0)
    def _():
        acc[...] = jnp.zeros_like(acc)

    @pl.when(k < nk // 2)
    def _():
        acc[...] += jnp.dot(lru_ref[...], wo_ref[...], preferred_element_type=F32)

    @pl.when(k >= nk // 2)
    def _():
        acc[...] += jnp.dot(ssd_ref[...], wo_ref[...], preferred_element_type=F32)

    @pl.when(k == nk - 1)
    def _():
        x1 = x_ref[...] + mod_ref[2:3, :] * acc[...]
        x1_ref[...] = x1
        y = x1 * lax.rsqrt(jnp.mean(x1 * x1, axis=-1, keepdims=True) + EPS) * g_ref[...]
        h2 = y * (1.0 + mod_ref[4:5, :]) + mod_ref[3:4, :]
        h2t_ref[...] = h2.T.astype(BF16)
        q_ref[...] = jnp.dot(h2.astype(BF16), wq_ref[...], preferred_element_type=F32)


def _outproj(lru, ssd, wo, x2d, mod3, g, wq, seq):
    t, d = x2d.shape
    half = lru.shape[1]
    tm = min(512, seq)
    tk = 1024
    nkh = half // tk
    per_batch = seq // tm
    nq = wq.shape[1]
    return pl.pallas_call(
        _outproj_kernel,
        grid=(t // tm, 2 * nkh),
        in_specs=[
            pl.BlockSpec((tm, tk), lambda i, k: (i, jnp.minimum(k, nkh - 1))),
            pl.BlockSpec((tm, tk), lambda i, k: (i, jnp.maximum(k - nkh, 0))),
            pl.BlockSpec((tk, d), lambda i, k: (k, 0)),
            pl.BlockSpec((tm, d), lambda i, k: (i, 0)),
            pl.BlockSpec((None, 6, d), lambda i, k: (i // per_batch, 0, 0)),
            pl.BlockSpec((1, d), lambda i, k: (0, 0)),
            pl.BlockSpec((d, nq), lambda i, k: (0, 0)),
        ],
        out_specs=[
            pl.BlockSpec((tm, d), lambda i, k: (i, 0)),
            pl.BlockSpec((d, tm), lambda i, k: (0, i)),
            pl.BlockSpec((tm, nq), lambda i, k: (i, 0)),
        ],
        out_shape=[
            jax.ShapeDtypeStruct((t, d), F32),
            jax.ShapeDtypeStruct((d, t), BF16),
            jax.ShapeDtypeStruct((t, nq), F32),
        ],
        scratch_shapes=[pltpu.VMEM((tm, d), F32)],
        compiler_params=_cparams(("parallel", "arbitrary")),
        name="outproj",
    )(lru, ssd, wo, x2d, mod3, g.reshape(1, d), wq)


def _sorting_network(n):
    pairs = []

    def merge(lo, hi, r):
        step = r * 2
        if step < hi - lo:
            merge(lo, hi, step)
            merge(lo + r, hi, step)
            pairs.extend((i, i + r) for i in range(lo + r, hi - r, step))
        else:
            pairs.append((lo, lo + r))

    def sort(lo, hi):
        if hi - lo >= 1:
            mid = lo + (hi - lo) // 2
            sort(lo, mid)
            sort(mid + 1, hi)
            merge(lo, hi, 1)

    sort(0, n - 1)
    return pairs


def _top_values(s, count):
    rows, tb = s.shape
    n = rows // SUBLANES
    assert n == count
    a = [s[k * SUBLANES:(k + 1) * SUBLANES, :] for k in range(n)]
    for i, j in _sorting_network(n):
        a[i], a[j] = jnp.maximum(a[i], a[j]), jnp.minimum(a[i], a[j])
    rid = lax.broadcasted_iota(jnp.int32, (count, tb), 0)
    out = jnp.zeros((count, tb), F32)
    for r in range(count):
        m = jnp.max(a[0], axis=0, keepdims=True)
        out = jnp.where(rid == r, m, out)
        hit = a[0] == m
        for k in range(count - 1 - r):
            a[k] = jnp.where(hit, a[k + 1], a[k])
    return out


def _pair_candidates(v1, v2):
    parts = [v1[0:1] + v2]
    parts += [v1[a:a + 1] + v2[0:8] for a in (1, 2, 3)]
    parts += [v1[0:8] + v2[b:b + 1] for b in (0, 1, 2)]
    parts.append(v1[8:16] + v2[0:1])
    return jnp.concatenate(parts, axis=0)


def _kth_and_expsum(cand, count):
    z = jnp.zeros((1, cand.shape[1]), F32)
    m = None
    for _ in range(count):
        m = jnp.max(cand, axis=0, keepdims=True)
        z = z + jnp.exp2(m)
        cand = jnp.where(cand == m, NEG_BIG, cand)
    return m, z


def _route_kernel(q_ref, keys_ref, s1_ref, s2_ref, tau_ref):
    for h in range(PEER_HEADS):
        ss = []
        for half in range(2):
            col = (2 * h + half) * LANES
            qh = q_ref[:, col:col + LANES].astype(BF16)
            s = lax.dot_general(keys_ref[h, half], qh, (((1,), (1,)), ((), ())), preferred_element_type=F32)
            ss.append(s - jnp.max(s, axis=0, keepdims=True))
        s1, s2 = ss
        v1 = _top_values(s1, PEER_TOPK)
        v2 = _top_values(s2, PEER_TOPK)
        _, z = _kth_and_expsum(_pair_candidates(v1, v2), PEER_TOPK)
        lz = jnp.log2(z)
        s1 = s1 - lz
        tau, _ = _kth_and_expsum(_pair_candidates(v1 - lz, v2), PEER_TOPK)
        s1_ref[h] = s1
        s2_ref[h] = s2
        tau_ref[h:h + 1, :] = tau


def _route(q, keys):
    t, nq = q.shape
    tb = min(512, t)
    nk = keys.shape[2]
    return pl.pallas_call(
        _route_kernel,
        grid=(t // tb,),
        in_specs=[
            pl.BlockSpec((tb, nq), lambda i: (i, 0)),
            pl.BlockSpec(keys.shape, lambda i: (0, 0, 0, 0)),
        ],
        out_specs=[
            pl.BlockSpec((PEER_HEADS, nk, tb), lambda i: (0, 0, i)),
            pl.BlockSpec((PEER_HEADS, nk, tb), lambda i: (0, 0, i)),
            pl.BlockSpec((PEER_HEADS, tb), lambda i: (0, i)),
        ],
        out_shape=[
            jax.ShapeDtypeStruct((PEER_HEADS, nk, t), F32),
            jax.ShapeDtypeStruct((PEER_HEADS, nk, t), F32),
            jax.ShapeDtypeStruct((PEER_HEADS, t), F32),
        ],
        compiler_params=_cparams(("parallel",)),
        name="route",
    )(q, keys)


def _vt_kernel(v_ref, o_ref):
    o_ref[...] = v_ref[...].T.astype(BF16)


def _transpose_bf16(v):
    ne, d = v.shape
    eb = 512
    return pl.pallas_call(
        _vt_kernel,
        grid=(ne // eb,),
        in_specs=[pl.BlockSpec((eb, d), lambda e: (e, 0))],
        out_specs=pl.BlockSpec((d, eb), lambda e: (0, e)),
        out_shape=jax.ShapeDtypeStruct((d, ne), BF16),
        compiler_params=_cparams(("parallel",)),
        name="vt",
    )(v)


def _peer_kernel(h2t_ref, u_ref, vt_ref, s1_ref, s2_ref, tau_ref, o_ref, at_scr, w_scr, *, tc):
    e = pl.program_id(1)
    ts = h2t_ref.shape[1]
    eb = u_ref.shape[0]

    @pl.when(e == 0)
    def _():
        o_ref[...] = jnp.zeros_like(o_ref)

    for c in range(ts // tc):
        cs = slice(c * tc, (c + 1) * tc)
        for it in range(eb // PEER_KEYS):
            rows = slice(it * PEER_KEYS, (it + 1) * PEER_KEYS)
            for lc in range(tc // LANES):
                ls = slice(c * tc + lc * LANES, c * tc + (lc + 1) * LANES)
                wsum = None
                for h in range(PEER_HEADS):
                    tsum = s1_ref[h, it:it + 1, ls] + s2_ref[h, :, ls]
                    term = jnp.where(tsum >= tau_ref[h:h + 1, ls], jnp.exp2(tsum), 0.0)
                    wsum = term if wsum is None else wsum + term
                w_scr[rows, lc * LANES:(lc + 1) * LANES] = wsum
        at_scr[...] = jnp.dot(u_ref[...], h2t_ref[:, cs], preferred_element_type=F32)
        g = (_gelu(at_scr[...]) * w_scr[...]).astype(BF16)
        o_ref[:, cs] += jnp.dot(vt_ref[...], g, preferred_element_type=F32)


def _peer(h2t, u, vt, s1, s2, tau):
    d, t = h2t.shape
    ne = u.shape[0]
    ts = min(512, t)
    tc = min(512, ts)
    eb = 1024
    return pl.pallas_call(
        functools.partial(_peer_kernel, tc=tc),
        grid=(t // ts, ne // eb),
        in_specs=[
            pl.BlockSpec((d, ts), lambda s, e: (0, s)),
            pl.BlockSpec((eb, d), lambda s, e: (e, 0)),
            pl.BlockSpec((d, eb), lambda s, e: (0, e)),
            pl.BlockSpec((PEER_HEADS, eb // PEER_KEYS, ts), lambda s, e: (0, e, s)),
            pl.BlockSpec((PEER_HEADS, PEER_KEYS, ts), lambda s, e: (0, 0, s)),
            pl.BlockSpec((PEER_HEADS, ts), lambda s, e: (0, s)),
        ],
        out_specs=pl.BlockSpec((d, ts), lambda s, e: (0, s)),
        out_shape=jax.ShapeDtypeStruct((d, t), F32),
        scratch_shapes=[pltpu.VMEM((eb, tc), F32), pltpu.VMEM((eb, tc), F32)],
        compiler_params=_cparams(("parallel", "arbitrary")),
        name="peer",
    )(h2t, u, vt, s1, s2, tau)


def _final_kernel(x1_ref, pt_ref, mod_ref, g_ref, o_ref):
    x2 = x1_ref[...] + mod_ref[5:6, :] * pt_ref[...].T
    o_ref[...] = x2 * lax.rsqrt(jnp.mean(x2 * x2, axis=-1, keepdims=True) + EPS) * g_ref[...]


def _final(x1, peer_t, mod3, g, seq):
    t, d = x1.shape
    tm = min(512, seq)
    per_batch = seq // tm
    return pl.pallas_call(
        _final_kernel,
        grid=(t // tm,),
        in_specs=[
            pl.BlockSpec((tm, d), lambda i: (i, 0)),
            pl.BlockSpec((d, tm), lambda i: (0, i)),
            pl.BlockSpec((None, 6, d), lambda i: (i // per_batch, 0, 0)),
            pl.BlockSpec((1, d), lambda i: (0, 0)),
        ],
        out_specs=pl.BlockSpec((tm, d), lambda i: (i, 0)),
        out_shape=jax.ShapeDtypeStruct((t, d), F32),
        compiler_params=_cparams(("parallel",)),
        name="final",
    )(x1, peer_t, mod3, g.reshape(1, d))


def _layer(x2d, mod3, batch, seq, norm1_g, w_in, lru_conv_w, lru_conv_b, lru_w_a, lru_b_a, lru_w_i, lru_b_i,
           lru_lambda, ssd_conv_w, ssd_conv_b, ssd_dt_bias, ssd_a_log, ssd_d, ssd_norm_g, w_out, norm2_g,
           peer_w_q, peer_sub_keys, peer_u, peer_v):
    d = x2d.shape[1]
    lru_w = lru_lambda.shape[0]
    ssd_w = ssd_norm_g.shape[0]
    xbc_w = ssd_conv_w.shape[1]
    n_heads = ssd_dt_bias.shape[0]
    o_lru_x = lru_w
    o_ssd_z = 2 * lru_w
    o_ssd_xbc = o_ssd_z + ssd_w
    o_ssd_dt = o_ssd_xbc + xbc_w

    w_bf = w_in.astype(BF16)
    wdt = jnp.pad(w_bf[:, o_ssd_dt:o_ssd_dt + n_heads], ((0, 0), (0, LANES - n_heads)))
    wab = jnp.concatenate([lru_w_a, lru_w_i], axis=-1).astype(BF16)
    pad_h = lambda v: jnp.pad(v, (0, LANES - n_heads)).reshape(1, LANES)
    head_of_channel = jnp.arange(ssd_w, dtype=jnp.int32) // SSD_HEAD_DIM
    sel = (jnp.arange(LANES, dtype=jnp.int32)[:, None] == head_of_channel[None, :]).astype(BF16)
    tri = (jnp.arange(SSD_CHUNK)[:, None] >= jnp.arange(SSD_CHUNK)[None, :]).astype(BF16)
    dsk = jnp.repeat(ssd_d, SSD_HEAD_DIM).reshape(1, ssd_w)

    h, dt_raw = _norm1(x2d, mod3, norm1_g, wdt, seq)
    proj = _inproj(h, w_bf, o_ssd_dt)
    lru = _lru(proj, lru_conv_w, lru_conv_b.reshape(1, lru_w), wab, lru_b_a.reshape(1, lru_w),
               lru_b_i.reshape(1, lru_w), lru_lambda.reshape(1, lru_w), batch, seq, lru_w, o_lru_x)
    ssd = _ssd(proj, dt_raw, ssd_conv_w, ssd_conv_b.reshape(1, xbc_w), pad_h(ssd_dt_bias), pad_h(ssd_a_log),
               dsk, ssd_norm_g.reshape(1, ssd_w), sel, tri, batch, seq, ssd_w, o_ssd_z, o_ssd_xbc)
    x1, h2t, q = _outproj(lru, ssd, w_out.astype(BF16), x2d, mod3, norm2_g, peer_w_q.astype(BF16), seq)
    s1, s2, tau = _route(q, (peer_sub_keys * math.log2(math.e)).astype(BF16))
    peer_t = _peer(h2t, peer_u.astype(BF16), _transpose_bf16(peer_v), s1, s2, tau)
    return x1, peer_t


def kernel(x, c, w_ada, b_ada, norm1_g, w_in, lru_conv_w, lru_conv_b, lru_w_a, lru_b_a, lru_w_i, lru_b_i, lru_lambda, ssd_conv_w, ssd_conv_b, ssd_dt_bias, ssd_a_log, ssd_d, ssd_norm_g, w_out, norm2_g, peer_w_q, peer_sub_keys, peer_u, peer_v, final_norm_g):
    batch, seq, d = x.shape
    depth = w_ada.shape[0]
    assert depth == 1, "the final residual is fused with the last layer; only depth 1 is wired up"
    x2d = x.reshape(batch * seq, d)
    l = 0
    mod3 = _ada(c, w_ada[l], b_ada[l]).reshape(batch, 6, d)
    x1, peer_t = _layer(x2d, mod3, batch, seq, norm1_g[l], w_in[l], lru_conv_w[l], lru_conv_b[l], lru_w_a[l],
                        lru_b_a[l], lru_w_i[l], lru_b_i[l], lru_lambda[l], ssd_conv_w[l], ssd_conv_b[l],
                        ssd_dt_bias[l], ssd_a_log[l], ssd_d[l], ssd_norm_g[l], w_out[l], norm2_g[l],
                        peer_w_q[l], peer_sub_keys[l], peer_u[l], peer_v[l])
    y = _final(x1, peer_t, mod3, final_norm_g, seq)
    return y.reshape(batch, seq, d)
```

```python
import functools
import math

import jax
import jax.numpy as jnp
from jax import lax
from jax.experimental import pallas as pl
from jax.experimental.pallas import tpu as pltpu

F32 = jnp.float32
BF16 = jnp.bfloat16

EPS = 1e-6
CONV_WIDTH = 4
LRU_C = 8.0
LRU_HEAD_DIM = 128
SSD_HEAD_DIM = 64
SSD_GROUPS = 4
SSD_STATE = 128
SSD_CHUNK = 128
PEER_HEADS = 8
PEER_KEYS = 128
PEER_TOPK = 16
LANES = 128
SUBLANES = 8
NEG_BIG = -1e30
VMEM_LIMIT = 56 * 1024 * 1024


def _cparams(sem, flags=None):
    return pltpu.CompilerParams(dimension_semantics=sem, vmem_limit_bytes=VMEM_LIMIT, flags=flags)


def _split3(x):
    hi = x.astype(BF16)
    r1 = x - hi.astype(F32)
    mid = r1.astype(BF16)
    lo = (r1 - mid.astype(F32)).astype(BF16)
    return hi, mid, lo


def _dot_sel_right(x, sel):
    return sum(jnp.dot(p, sel, preferred_element_type=F32) for p in _split3(x))


def _dot_sel_left(sel, x):
    return sum(jnp.dot(sel, p, preferred_element_type=F32) for p in _split3(x))


def _softplus(x):
    return jnp.maximum(x, 0.0) + jnp.log1p(jnp.exp(-jnp.abs(x)))


def _silu(x):
    return x * jax.nn.sigmoid(x)


def _causal_conv(xpad, cw_ref, cb_ref, rows):
    e = xpad[0:SUBLANES + rows, :]
    z = cw_ref[0:1, :] * e
    for k in range(1, CONV_WIDTH):
        z = pltpu.roll(z, 1, 0) + cw_ref[k:k + 1, :] * e
    return z[SUBLANES:, :] + cb_ref[...]


def _gelu(x):
    c = math.sqrt(2.0 / math.pi)
    hx = 0.5 * x
    return hx + hx * jnp.tanh(x * (c + (c * 0.044715) * (x * x)))


def _ada_kernel(c_ref, w_ref, b_ref, o_ref, act_ref):
    nb = c_ref.shape[0]
    tn = w_ref.shape[1]

    @pl.when(pl.program_id(0) == 0)
    def _():
        cv = c_ref[...]
        act_ref[...] = _silu(cv)

    for k in range(tn // LANES):
        sl = slice(k * LANES, (k + 1) * LANES)
        wk = w_ref[:, sl]
        for b in range(nb):
            o_ref[b:b + 1, sl] = jnp.sum(wk * act_ref[b], axis=0, keepdims=True) + b_ref[:, sl]


def _ada(c, w_ada, b_ada):
    nb, d = c.shape
    n = w_ada.shape[1]
    tn = 1024
    cb = jnp.broadcast_to(c[:, :, None], (nb, d, LANES))
    return pl.pallas_call(
        _ada_kernel,
        grid=(n // tn,),
        in_specs=[
            pl.BlockSpec((nb, d, LANES), lambda j: (0, 0, 0)),
            pl.BlockSpec((d, tn), lambda j: (0, j)),
            pl.BlockSpec((1, tn), lambda j: (0, j)),
        ],
        out_specs=pl.BlockSpec((nb, tn), lambda j: (0, j)),
        out_shape=jax.ShapeDtypeStruct((nb, n), F32),
        scratch_shapes=[pltpu.VMEM((nb, d, LANES), F32)],
        compiler_params=_cparams(("arbitrary",)),
        name="ada",
    )(cb, w_ada, b_ada.reshape(1, n))


def _norm1_kernel(x_ref, mod_ref, g_ref, wdt_ref, h_ref, dt_ref):
    x = x_ref[...]
    y = x * lax.rsqrt(jnp.mean(x * x, axis=-1, keepdims=True) + EPS) * g_ref[...]
    h = (y * (1.0 + mod_ref[1:2, :]) + mod_ref[0:1, :]).astype(BF16)
    h_ref[...] = h
    dt_ref[...] = jnp.dot(h, wdt_ref[...], preferred_element_type=F32)


def _norm1(x2d, mod3, g, wdt, seq):
    t, d = x2d.shape
    tm = min(512, seq)
    per_batch = seq // tm
    return pl.pallas_call(
        _norm1_kernel,
        grid=(t // tm,),
        in_specs=[
            pl.BlockSpec((tm, d), lambda i: (i, 0)),
            pl.BlockSpec((None, 6, d), lambda i: (i // per_batch, 0, 0)),
            pl.BlockSpec((1, d), lambda i: (0, 0)),
            pl.BlockSpec((d, LANES), lambda i: (0, 0)),
        ],
        out_specs=[
            pl.BlockSpec((tm, d), lambda i: (i, 0)),
            pl.BlockSpec((tm, LANES), lambda i: (i, 0)),
        ],
        out_shape=[jax.ShapeDtypeStruct((t, d), BF16), jax.ShapeDtypeStruct((t, LANES), F32)],
        compiler_params=_cparams(("parallel",)),
        name="norm1",
    )(x2d, mod3, g.reshape(1, d), wdt)


def _inproj_kernel(h_ref, w_ref, o_ref):
    o_ref[...] = jnp.dot(h_ref[...], w_ref[...], preferred_element_type=F32)


def _inproj(h, w_bf, n_main):
    t, d = h.shape
    tn = 1152
    tm = min(2048, t)
    return pl.pallas_call(
        _inproj_kernel,
        grid=(n_main // tn, t // tm),
        in_specs=[
            pl.BlockSpec((tm, d), lambda j, i: (i, 0)),
            pl.BlockSpec((d, tn), lambda j, i: (0, j)),
        ],
        out_specs=pl.BlockSpec((tm, tn), lambda j, i: (i, j)),
        out_shape=jax.ShapeDtypeStruct((t, n_main), F32),
        compiler_params=_cparams(("parallel", "arbitrary")),
        name="inproj",
    )(h, w_bf)


def _lru_kernel(g_ref, x_ref, cw_ref, cb_ref, wab_ref, ba_ref, bi_ref, lam_ref, o_ref,
                xpad, a_scr, b_scr, hs, hcar):
    tt, c = x_ref.shape
    t = pl.program_id(2)

    @pl.when(t == 0)
    def _():
        xpad[0:SUBLANES, :] = jnp.zeros((SUBLANES, c), F32)
        hcar[...] = jnp.zeros_like(hcar)

    xpad[SUBLANES:SUBLANES + tt, :] = x_ref[...]
    y = _causal_conv(xpad, cw_ref, cb_ref, tt)
    xpad[0:SUBLANES, :] = xpad[tt:tt + SUBLANES, :]

    row = lax.broadcasted_iota(jnp.int32, (tt // SUBLANES, SUBLANES, LRU_HEAD_DIM), 1)
    for j in range(c // LRU_HEAD_DIM):
        sl = slice(j * LRU_HEAD_DIM, (j + 1) * LRU_HEAD_DIM)
        yj = y[:, sl]
        gm = jnp.dot(yj.astype(BF16), wab_ref[j], preferred_element_type=F32)
        r = jax.nn.sigmoid(gm[:, :LRU_HEAD_DIM] + ba_ref[:, sl])
        i = jax.nn.sigmoid(gm[:, LRU_HEAD_DIM:] + bi_ref[:, sl])
        log_a = (-LRU_C * _softplus(-lam_ref[:, sl])) * r
        a = jnp.exp(log_a)
        th = jnp.tanh(log_a)
        b = jnp.sqrt((-2.0 * th) / (1.0 - th)) * (i * yj)
        a = a.reshape(tt // SUBLANES, SUBLANES, LRU_HEAD_DIM)
        b = b.reshape(tt // SUBLANES, SUBLANES, LRU_HEAD_DIM)
        for s in (1, 2, 4):
            a_sh = pltpu.roll(a, s, 1)
            b_sh = pltpu.roll(b, s, 1)
            m = row >= s
            b = jnp.where(m, a * b_sh + b, b)
            a = jnp.where(m, a * a_sh, a)
        a_scr[:, sl] = a.reshape(tt, LRU_HEAD_DIM)
        b_scr[:, sl] = b.reshape(tt, LRU_HEAD_DIM)

    carry = hcar[...]
    for n in range(tt // SUBLANES):
        rs = slice(n * SUBLANES, (n + 1) * SUBLANES)
        hb = b_scr[rs, :] + a_scr[rs, :] * carry
        hs[rs, :] = hb
        carry = hb[SUBLANES - 1:SUBLANES, :]
    hcar[...] = carry
    o_ref[...] = (_gelu(g_ref[...]) * hs[...]).astype(BF16)


def _lru(proj, cw, cb, wab, ba, bi, lam, batch, seq, width, x_col0):
    t = proj.shape[0]
    c = 512
    tt = min(512, seq)
    nt = seq // tt
    ncb = width // c
    hpb = c // LRU_HEAD_DIM
    xoff = x_col0 // c
    return pl.pallas_call(
        _lru_kernel,
        grid=(batch, ncb, nt),
        in_specs=[
            pl.BlockSpec((tt, c), lambda b, j, i: (b * nt + i, j)),
            pl.BlockSpec((tt, c), lambda b, j, i: (b * nt + i, xoff + j)),
            pl.BlockSpec((CONV_WIDTH, c), lambda b, j, i: (0, j)),
            pl.BlockSpec((1, c), lambda b, j, i: (0, j)),
            pl.BlockSpec((hpb, LRU_HEAD_DIM, 2 * LRU_HEAD_DIM), lambda b, j, i: (j, 0, 0)),
            pl.BlockSpec((1, c), lambda b, j, i: (0, j)),
            pl.BlockSpec((1, c), lambda b, j, i: (0, j)),
            pl.BlockSpec((1, c), lambda b, j, i: (0, j)),
        ],
        out_specs=pl.BlockSpec((tt, c), lambda b, j, i: (b * nt + i, j)),
        out_shape=jax.ShapeDtypeStruct((t, width), BF16),
        scratch_shapes=[
            pltpu.VMEM((tt + SUBLANES, c), F32),
            pltpu.VMEM((tt, c), F32),
            pltpu.VMEM((tt, c), F32),
            pltpu.VMEM((tt, c), F32),
            pltpu.VMEM((1, c), F32),
        ],
        compiler_params=_cparams(("parallel", "parallel", "arbitrary")),
        name="lru",
    )(proj, proj, cw, cb, wab, ba, bi, lam)


def _ssd_kernel(z_ref, xbc_ref, dt_ref, cw_ref, cb_ref, dtb_ref, alog_ref, dsk_ref, ng_ref, e_ref, tri_ref,
                o_ref, xpad, st):
    L = SSD_CHUNK
    N = SSD_STATE
    w = z_ref.shape[1]
    gw = w // SSD_GROUPS
    gn = SSD_GROUPS * N

    @pl.when(pl.program_id(1) == 0)
    def _():
        xpad[0:SUBLANES, :] = jnp.zeros((SUBLANES, xpad.shape[1]), F32)
        st[...] = jnp.zeros_like(st)

    xpad[SUBLANES:SUBLANES + L, :] = xbc_ref[...]
    y = _causal_conv(xpad, cw_ref, cb_ref, L)
    xpad[0:SUBLANES, :] = xpad[L:L + SUBLANES, :]
    act = _silu(y)
    xs = act[:, :w]
    bm = act[:, w:w + gn]
    cm = act[:, w + gn:]

    dt = _softplus(dt_ref[...] + dtb_ref[...])
    da = dt * (-jnp.exp(alog_ref[...]))
    a_cs = _dot_sel_left(tri_ref[...], da)
    a_cs_t = a_cs.T
    ea = jnp.exp(a_cs)
    dte = jnp.exp(a_cs[L - 1:L, :] - a_cs)
    sel = e_ref[...]
    dt_f = _dot_sel_right(dt, sel)
    ea_f = _dot_sel_right(ea, sel)
    dte_f = _dot_sel_right(dte, sel)
    xc = xs * dt_f
    xcb = xc.astype(BF16)
    xd = (xc * dte_f).astype(BF16)

    causal = lax.broadcasted_iota(jnp.int32, (L, L), 0) >= lax.broadcasted_iota(jnp.int32, (L, L), 1)
    lane = lax.broadcasted_iota(jnp.int32, (L, LANES), 1)
    low_half = lane < SSD_HEAD_DIM
    heads_per_group = gw // SSD_HEAD_DIM
    for g in range(SSD_GROUPS):
        gs = slice(g * gw, (g + 1) * gw)
        cg = cm[:, g * N:(g + 1) * N].astype(BF16)
        bg = bm[:, g * N:(g + 1) * N]
        cbm = lax.dot_general(cg, bg.astype(BF16), (((1,), (1,)), ((), ())), preferred_element_type=F32)
        bg_t = bg.T.astype(BF16)
        s_prev = st[g]
        y_off = jnp.dot(cg, s_prev.astype(BF16), preferred_element_type=F32) * ea_f[:, gs]
        s_new = jnp.dot(bg_t, xd[:, gs], preferred_element_type=F32)
        st[g] = ea_f[L - 1:L, gs] * s_prev + s_new
        for p in range(heads_per_group // 2):
            h0 = g * heads_per_group + 2 * p
            cs = slice(h0 * SSD_HEAD_DIM, (h0 + 2) * SSD_HEAD_DIM)
            ms = []
            for hh in (h0, h0 + 1):
                seg = a_cs[:, hh:hh + 1] - a_cs_t[hh:hh + 1, :]
                ms.append(cbm * jnp.exp(jnp.where(causal, seg, NEG_BIG)))
            lhs = jnp.concatenate(ms, axis=1).astype(BF16)
            pair = xcb[:, cs]
            zero = jnp.zeros_like(pair)
            rhs = jnp.concatenate([jnp.where(low_half, pair, zero), jnp.where(low_half, zero, pair)], axis=0)
            y_diag = jnp.dot(lhs, rhs, preferred_element_type=F32)
            yp = y_diag + y_off[:, 2 * p * SSD_HEAD_DIM:(2 * p + 2) * SSD_HEAD_DIM] + dsk_ref[:, cs] * xs[:, cs]
            xpad[SUBLANES:SUBLANES + L, cs] = yp

    yz = xpad[SUBLANES:SUBLANES + L, 0:w] * _silu(z_ref[...])
    for g in range(SSD_GROUPS):
        gs = slice(g * gw, (g + 1) * gw)
        v = yz[:, gs]
        v = v * lax.rsqrt(jnp.mean(v * v, axis=-1, keepdims=True) + EPS)
        o_ref[:, gs] = (v * ng_ref[:, gs]).astype(BF16)


def _ssd(proj, dt_raw, cw, cb, dtb, alog, dsk, ng, sel, tri, batch, seq, width, z_col0, xbc_col0):
    t = proj.shape[0]
    xw = cw.shape[1]
    L = SSD_CHUNK
    nc = seq // L
    zoff = z_col0 // width
    xoff = xbc_col0 // xw
    return pl.pallas_call(
        _ssd_kernel,
        grid=(batch, nc),
        in_specs=[
            pl.BlockSpec((L, width), lambda b, i: (b * nc + i, zoff)),
            pl.BlockSpec((L, xw), lambda b, i: (b * nc + i, xoff)),
            pl.BlockSpec((L, LANES), lambda b, i: (b * nc + i, 0)),
            pl.BlockSpec((CONV_WIDTH, xw), lambda b, i: (0, 0)),
            pl.BlockSpec((1, xw), lambda b, i: (0, 0)),
            pl.BlockSpec((1, LANES), lambda b, i: (0, 0)),
            pl.BlockSpec((1, LANES), lambda b, i: (0, 0)),
            pl.BlockSpec((1, width), lambda b, i: (0, 0)),
            pl.BlockSpec((1, width), lambda b, i: (0, 0)),
            pl.BlockSpec((LANES, width), lambda b, i: (0, 0)),
            pl.BlockSpec((L, L), lambda b, i: (0, 0)),
        ],
        out_specs=pl.BlockSpec((L, width), lambda b, i: (b * nc + i, 0)),
        out_shape=jax.ShapeDtypeStruct((t, width), BF16),
        scratch_shapes=[
            pltpu.VMEM((L + SUBLANES, xw), F32),
            pltpu.VMEM((SSD_GROUPS, SSD_STATE, width // SSD_GROUPS), F32),
        ],
        compiler_params=_cparams(("parallel", "arbitrary")),
        name="ssd",
    )(proj, proj, dt_raw, cw, cb, dtb, alog, dsk, ng, sel, tri)


def _outproj_kernel(lru_ref, ssd_ref, wo_ref, x_ref, mod_ref, g_ref, wq_ref, x1_ref, h2t_ref, q_ref, acc):
    k = pl.program_id(1)
    nk = pl.num_programs(1)

    @pl.when(k == 0)
    def _():
        acc[...] = jnp.zeros_like(acc)

    @pl.when(k < nk // 2)
    def _():
        acc[...] += jnp.dot(lru_ref[...], wo_ref[...], preferred_element_type=F32)

    @pl.when(k >= nk // 2)
    def _():
        acc[...] += jnp.dot(ssd_ref[...], wo_ref[...], preferred_element_type=F32)

    @pl.when(k == nk - 1)
    def _():
        x1 = x_ref[...] + mod_ref[2:3, :] * acc[...]
        x1_ref[...] = x1
        y = x1 * lax.rsqrt(jnp.mean(x1 * x1, axis=-1, keepdims=True) + EPS) * g_ref[...]
        h2 = y * (1.0 + mod_ref[4:5, :]) + mod_ref[3:4, :]
        h2t_ref[...] = h2.T.astype(BF16)
        q_ref[...] = jnp.dot(h2.astype(BF16), wq_ref[...], preferred_element_type=F32)


def _outproj(lru, ssd, wo, x2d, mod3, g, wq, seq):
    t, d = x2d.shape
    half = lru.shape[1]
    tm = min(512, seq)
    tk = 1024
    nkh = half // tk
    per_batch = seq // tm
    nq = wq.shape[1]
    return pl.pallas_call(
        _outproj_kernel,
        grid=(t // tm, 2 * nkh),
        in_specs=[
            pl.BlockSpec((tm, tk), lambda i, k: (i, jnp.minimum(k, nkh - 1))),
            pl.BlockSpec((tm, tk), lambda i, k: (i, jnp.maximum(k - nkh, 0))),
            pl.BlockSpec((tk, d), lambda i, k: (k, 0)),
            pl.BlockSpec((tm, d), lambda i, k: (i, 0)),
            pl.BlockSpec((None, 6, d), lambda i, k: (i // per_batch, 0, 0)),
            pl.BlockSpec((1, d), lambda i, k: (0, 0)),
            pl.BlockSpec((d, nq), lambda i, k: (0, 0)),
        ],
        out_specs=[
            pl.BlockSpec((tm, d), lambda i, k: (i, 0)),
            pl.BlockSpec((d, tm), lambda i, k: (0, i)),
            pl.BlockSpec((tm, nq), lambda i, k: (i, 0)),
        ],
        out_shape=[
            jax.ShapeDtypeStruct((t, d), F32),
            jax.ShapeDtypeStruct((d, t), BF16),
            jax.ShapeDtypeStruct((t, nq), F32),
        ],
        scratch_shapes=[pltpu.VMEM((tm, d), F32)],
        compiler_params=_cparams(("parallel", "arbitrary")),
        name="outproj",
    )(lru, ssd, wo, x2d, mod3, g.reshape(1, d), wq)


def _sorting_network(n):
    pairs = []

    def merge(lo, hi, r):
        step = r * 2
        if step < hi - lo:
            merge(lo, hi, step)
            merge(lo + r, hi, step)
            pairs.extend((i, i + r) for i in range(lo + r, hi - r, step))
        else:
            pairs.append((lo, lo + r))

    def sort(lo, hi):
        if hi - lo >= 1:
            mid = lo + (hi - lo) // 2
            sort(lo, mid)
            sort(mid + 1, hi)
            merge(lo, hi, 1)

    sort(0, n - 1)
    return pairs


def _top_values(s, count):
    rows, tb = s.shape
    n = rows // SUBLANES
    assert n == count
    a = [s[k * SUBLANES:(k + 1) * SUBLANES, :] for k in range(n)]
    for i, j in _sorting_network(n):
        a[i], a[j] = jnp.maximum(a[i], a[j]), jnp.minimum(a[i], a[j])
    rid = lax.broadcasted_iota(jnp.int32, (count, tb), 0)
    out = jnp.zeros((count, tb), F32)
    for r in range(count):
        m = jnp.max(a[0], axis=0, keepdims=True)
        out = jnp.where(rid == r, m, out)
        hit = a[0] == m
        for k in range(count - 1 - r):
            a[k] = jnp.where(hit, a[k + 1], a[k])
    return out


def _pair_candidates(v1, v2):
    parts = [v1[0:1] + v2]
    parts += [v1[a:a + 1] + v2[0:8] for a in (1, 2, 3)]
    parts += [v1[0:8] + v2[b:b + 1] for b in (0, 1, 2)]
    parts.append(v1[8:16] + v2[0:1])
    return jnp.concatenate(parts, axis=0)


def _kth_and_expsum(cand, count):
    z = jnp.zeros((1, cand.shape[1]), F32)
    m = None
    for _ in range(count):
        m = jnp.max(cand, axis=0, keepdims=True)
        z = z + jnp.exp2(m)
        cand = jnp.where(cand == m, NEG_BIG, cand)
    return m, z


def _route_kernel(q_ref, keys_ref, s1_ref, s2_ref, tau_ref):
    for h in range(PEER_HEADS):
        ss = []
        for half in range(2):
            col = (2 * h + half) * LANES
            qh = q_ref[:, col:col + LANES].astype(BF16)
            s = lax.dot_general(keys_ref[h, half], qh, (((1,), (1,)), ((), ())), preferred_element_type=F32)
            ss.append(s - jnp.max(s, axis=0, keepdims=True))
        s1, s2 = ss
        v1 = _top_values(s1, PEER_TOPK)
        v2 = _top_values(s2, PEER_TOPK)
        _, z = _kth_and_expsum(_pair_candidates(v1, v2), PEER_TOPK)
        lz = jnp.log2(z)
        s1 = s1 - lz
        tau, _ = _kth_and_expsum(_pair_candidates(v1 - lz, v2), PEER_TOPK)
        s1_ref[h] = s1
        s2_ref[h] = s2
        tau_ref[h:h + 1, :] = tau


def _route(q, keys):
    t, nq = q.shape
    tb = min(512, t)
    nk = keys.shape[2]
    return pl.pallas_call(
        _route_kernel,
        grid=(t // tb,),
        in_specs=[
            pl.BlockSpec((tb, nq), lambda i: (i, 0)),
            pl.BlockSpec(keys.shape, lambda i: (0, 0, 0, 0)),
        ],
        out_specs=[
            pl.BlockSpec((PEER_HEADS, nk, tb), lambda i: (0, 0, i)),
            pl.BlockSpec((PEER_HEADS, nk, tb), lambda i: (0, 0, i)),
            pl.BlockSpec((PEER_HEADS, tb), lambda i: (0, i)),
        ],
        out_shape=[
            jax.ShapeDtypeStruct((PEER_HEADS, nk, t), F32),
            jax.ShapeDtypeStruct((PEER_HEADS, nk, t), F32),
            jax.ShapeDtypeStruct((PEER_HEADS, t), F32),
        ],
        compiler_params=_cparams(("parallel",)),
        name="route",
    )(q, keys)


def _vt_kernel(v_ref, o_ref):
    o_ref[...] = v_ref[...].T.astype(BF16)


def _transpose_bf16(v):
    ne, d = v.shape
    eb = 512
    return pl.pallas_call(
        _vt_kernel,
        grid=(ne // eb,),
        in_specs=[pl.BlockSpec((eb, d), lambda e: (e, 0))],
        out_specs=pl.BlockSpec((d, eb), lambda e: (0, e)),
        out_shape=jax.ShapeDtypeStruct((d, ne), BF16),
        compiler_params=_cparams(("parallel",)),
        name="vt",
    )(v)


def _peer_kernel(h2t_ref, u_ref, vt_ref, s1_ref, s2_ref, tau_ref, o_ref, at_scr, w_scr, *, tc):
    e = pl.program_id(1)
    ts = h2t_ref.shape[1]
    eb = u_ref.shape[0]

    @pl.when(e == 0)
    def _():
        o_ref[...] = jnp.zeros_like(o_ref)

    for c in range(ts // tc):
        cs = slice(c * tc, (c + 1) * tc)
        for it in range(eb // PEER_KEYS):
            rows = slice(it * PEER_KEYS, (it + 1) * PEER_KEYS)
            for lc in range(tc // LANES):
                ls = slice(c * tc + lc * LANES, c * tc + (lc + 1) * LANES)
                wsum = None
                for h in range(PEER_HEADS):
                    tsum = s1_ref[h, it:it + 1, ls] + s2_ref[h, :, ls]
                    term = jnp.where(tsum >= tau_ref[h:h + 1, ls], jnp.exp2(tsum), 0.0)
                    wsum = term if wsum is None else wsum + term
                w_scr[rows, lc * LANES:(lc + 1) * LANES] = wsum
        at_scr[...] = jnp.dot(u_ref[...].astype(BF16), h2t_ref[:, cs], preferred_element_type=F32)
        g = (_gelu(at_scr[...]) * w_scr[...]).astype(BF16)
        o_ref[:, cs] += jnp.dot(vt_ref[...], g, preferred_element_type=F32)


def _peer(h2t, u, vt, s1, s2, tau):
    d, t = h2t.shape
    ne = u.shape[0]
    ts = min(512, t)
    tc = min(512, ts)
    eb = 1024
    return pl.pallas_call(
        functools.partial(_peer_kernel, tc=tc),
        grid=(t // ts, ne // eb),
        in_specs=[
            pl.BlockSpec((d, ts), lambda s, e: (0, s)),
            pl.BlockSpec((eb, d), lambda s, e: (e, 0)),
            pl.BlockSpec((d, eb), lambda s, e: (0, e)),
            pl.BlockSpec((PEER_HEADS, eb // PEER_KEYS, ts), lambda s, e: (0, e, s)),
            pl.BlockSpec((PEER_HEADS, PEER_KEYS, ts), lambda s, e: (0, 0, s)),
            pl.BlockSpec((PEER_HEADS, ts), lambda s, e: (0, s)),
        ],
        out_specs=pl.BlockSpec((d, ts), lambda s, e: (0, s)),
        out_shape=jax.ShapeDtypeStruct((d, t), F32),
        scratch_shapes=[pltpu.VMEM((eb, tc), F32), pltpu.VMEM((eb, tc), F32)],
        compiler_params=_cparams(("parallel", "arbitrary")),
        name="peer",
    )(h2t, u, vt, s1, s2, tau)


def _final_kernel(x1_ref, pt_ref, mod_ref, g_ref, o_ref):
    x2 = x1_ref[...] + mod_ref[5:6, :] * pt_ref[...].T
    o_ref[...] = x2 * lax.rsqrt(jnp.mean(x2 * x2, axis=-1, keepdims=True) + EPS) * g_ref[...]


def _final(x1, peer_t, mod3, g, seq):
    t, d = x1.shape
    tm = min(512, seq)
    per_batch = seq // tm
    return pl.pallas_call(
        _final_kernel,
        grid=(t // tm,),
        in_specs=[
            pl.BlockSpec((tm, d), lambda i: (i, 0)),
            pl.BlockSpec((d, tm), lambda i: (0, i)),
            pl.BlockSpec((None, 6, d), lambda i: (i // per_batch, 0, 0)),
            pl.BlockSpec((1, d), lambda i: (0, 0)),
        ],
        out_specs=pl.BlockSpec((tm, d), lambda i: (i, 0)),
        out_shape=jax.ShapeDtypeStruct((t, d), F32),
        compiler_params=_cparams(("parallel",)),
        name="final",
    )(x1, peer_t, mod3, g.reshape(1, d))


def _layer(x2d, mod3, batch, seq, norm1_g, w_in, lru_conv_w, lru_conv_b, lru_w_a, lru_b_a, lru_w_i, lru_b_i,
           lru_lambda, ssd_conv_w, ssd_conv_b, ssd_dt_bias, ssd_a_log, ssd_d, ssd_norm_g, w_out, norm2_g,
           peer_w_q, peer_sub_keys, peer_u, peer_v):
    d = x2d.shape[1]
    lru_w = lru_lambda.shape[0]
    ssd_w = ssd_norm_g.shape[0]
    xbc_w = ssd_conv_w.shape[1]
    n_heads = ssd_dt_bias.shape[0]
    o_lru_x = lru_w
    o_ssd_z = 2 * lru_w
    o_ssd_xbc = o_ssd_z + ssd_w
    o_ssd_dt = o_ssd_xbc + xbc_w

    w_bf = w_in.astype(BF16)
    wdt = jnp.pad(w_bf[:, o_ssd_dt:o_ssd_dt + n_heads], ((0, 0), (0, LANES - n_heads)))
    wab = jnp.concatenate([lru_w_a, lru_w_i], axis=-1).astype(BF16)
    pad_h = lambda v: jnp.pad(v, (0, LANES - n_heads)).reshape(1, LANES)
    head_of_channel = jnp.arange(ssd_w, dtype=jnp.int32) // SSD_HEAD_DIM
    sel = (jnp.arange(LANES, dtype=jnp.int32)[:, None] == head_of_channel[None, :]).astype(BF16)
    tri = (jnp.arange(SSD_CHUNK)[:, None] >= jnp.arange(SSD_CHUNK)[None, :]).astype(BF16)
    dsk = jnp.repeat(ssd_d, SSD_HEAD_DIM).reshape(1, ssd_w)

    h, dt_raw = _norm1(x2d, mod3, norm1_g, wdt, seq)
    proj = _inproj(h, w_bf, o_ssd_dt)
    lru = _lru(proj, lru_conv_w, lru_conv_b.reshape(1, lru_w), wab, lru_b_a.reshape(1, lru_w),
               lru_b_i.reshape(1, lru_w), lru_lambda.reshape(1, lru_w), batch, seq, lru_w, o_lru_x)
    ssd = _ssd(proj, dt_raw, ssd_conv_w, ssd_conv_b.reshape(1, xbc_w), pad_h(ssd_dt_bias), pad_h(ssd_a_log),
               dsk, ssd_norm_g.reshape(1, ssd_w), sel, tri, batch, seq, ssd_w, o_ssd_z, o_ssd_xbc)
    x1, h2t, q = _outproj(lru, ssd, w_out.astype(BF16), x2d, mod3, norm2_g, peer_w_q.astype(BF16), seq)
    s1, s2, tau = _route(q, (peer_sub_keys * math.log2(math.e)).astype(BF16))
    peer_t = _peer(h2t, peer_u, _transpose_bf16(peer_v), s1, s2, tau)
    return x1, peer_t


def kernel(x, c, w_ada, b_ada, norm1_g, w_in, lru_conv_w, lru_conv_b, lru_w_a, lru_b_a, lru_w_i, lru_b_i, lru_lambda, ssd_conv_w, ssd_conv_b, ssd_dt_bias, ssd_a_log, ssd_d, ssd_norm_g, w_out, norm2_g, peer_w_q, peer_sub_keys, peer_u, peer_v, final_norm_g):
    batch, seq, d = x.shape
    depth = w_ada.shape[0]
    assert depth == 1, "the final residual is fused with the last layer; only depth 1 is wired up"
    x2d = x.reshape(batch * seq, d)
    l = 0
    mod3 = _ada(c, w_ada[l], b_ada[l]).reshape(batch, 6, d)
    x1, peer_t = _layer(x2d, mod3, batch, seq, norm1_g[l], w_in[l], lru_conv_w[l], lru_conv_b[l], lru_w_a[l],
                        lru_b_a[l], lru_w_i[l], lru_b_i[l], lru_lambda[l], ssd_conv_w[l], ssd_conv_b[l],
                        ssd_dt_bias[l], ssd_a_log[l], ssd_d[l], ssd_norm_g[l], w_out[l], norm2_g[l],
                        peer_w_q[l], peer_sub_keys[l], peer_u[l], peer_v[l])
    y = _final(x1, peer_t, mod3, final_norm_g, seq)
    return y.reshape(batch, seq, d)
```

```python
import functools
import math

import jax
import jax.numpy as jnp
from jax import lax
from jax.experimental import pallas as pl
from jax.experimental.pallas import tpu as pltpu

F32 = jnp.float32
BF16 = jnp.bfloat16

EPS = 1e-6
CONV_WIDTH = 4
LRU_C = 8.0
LRU_HEAD_DIM = 128
SSD_HEAD_DIM = 64
SSD_GROUPS = 4
SSD_STATE = 128
SSD_CHUNK = 128
PEER_HEADS = 8
PEER_KEYS = 128
PEER_TOPK = 16
LANES = 128
SUBLANES = 8
NEG_BIG = -1e30
VMEM_LIMIT = 56 * 1024 * 1024


def _cparams(sem, flags=None):
    return pltpu.CompilerParams(dimension_semantics=sem, vmem_limit_bytes=VMEM_LIMIT, flags=flags)


def _split3(x):
    hi = x.astype(BF16)
    r1 = x - hi.astype(F32)
    mid = r1.astype(BF16)
    lo = (r1 - mid.astype(F32)).astype(BF16)
    return hi, mid, lo


def _dot_sel_right(x, sel):
    return sum(jnp.dot(p, sel, preferred_element_type=F32) for p in _split3(x))


def _dot_sel_left(sel, x):
    return sum(jnp.dot(sel, p, preferred_element_type=F32) for p in _split3(x))


def _softplus(x):
    return jnp.maximum(x, 0.0) + jnp.log1p(jnp.exp(-jnp.abs(x)))


def _silu(x):
    return x * jax.nn.sigmoid(x)


def _causal_conv(xpad, cw_ref, cb_ref, rows):
    e = xpad[0:SUBLANES + rows, :]
    z = cw_ref[0:1, :] * e
    for k in range(1, CONV_WIDTH):
        z = pltpu.roll(z, 1, 0) + cw_ref[k:k + 1, :] * e
    return z[SUBLANES:, :] + cb_ref[...]


def _gelu(x):
    c = math.sqrt(2.0 / math.pi)
    hx = 0.5 * x
    return hx + hx * jnp.tanh(x * (c + (c * 0.044715) * (x * x)))


def _ada_kernel(c_ref, w_ref, b_ref, o_ref, act_ref):
    nb = c_ref.shape[0]
    tn = w_ref.shape[1]

    @pl.when(pl.program_id(0) == 0)
    def _():
        cv = c_ref[...]
        act_ref[...] = _silu(cv)

    for k in range(tn // LANES):
        sl = slice(k * LANES, (k + 1) * LANES)
        wk = w_ref[:, sl]
        for b in range(nb):
            o_ref[b:b + 1, sl] = jnp.sum(wk * act_ref[b], axis=0, keepdims=True) + b_ref[:, sl]


def _ada(c, w_ada, b_ada):
    nb, d = c.shape
    n = w_ada.shape[1]
    tn = 1024
    cb = jnp.broadcast_to(c[:, :, None], (nb, d, LANES))
    return pl.pallas_call(
        _ada_kernel,
        grid=(n // tn,),
        in_specs=[
            pl.BlockSpec((nb, d, LANES), lambda j: (0, 0, 0)),
            pl.BlockSpec((d, tn), lambda j: (0, j)),
            pl.BlockSpec((1, tn), lambda j: (0, j)),
        ],
        out_specs=pl.BlockSpec((nb, tn), lambda j: (0, j)),
        out_shape=jax.ShapeDtypeStruct((nb, n), F32),
        scratch_shapes=[pltpu.VMEM((nb, d, LANES), F32)],
        compiler_params=_cparams(("arbitrary",)),
        name="ada",
    )(cb, w_ada, b_ada.reshape(1, n))


def _norm1_kernel(x_ref, mod_ref, g_ref, wdt_ref, h_ref, dt_ref):
    x = x_ref[...]
    y = x * lax.rsqrt(jnp.mean(x * x, axis=-1, keepdims=True) + EPS) * g_ref[...]
    h = (y * (1.0 + mod_ref[1:2, :]) + mod_ref[0:1, :]).astype(BF16)
    h_ref[...] = h
    dt_ref[...] = jnp.dot(h, wdt_ref[...], preferred_element_type=F32)


def _norm1(x2d, mod3, g, wdt, seq):
    t, d = x2d.shape
    tm = min(512, seq)
    per_batch = seq // tm
    return pl.pallas_call(
        _norm1_kernel,
        grid=(t // tm,),
        in_specs=[
            pl.BlockSpec((tm, d), lambda i: (i, 0)),
            pl.BlockSpec((None, 6, d), lambda i: (i // per_batch, 0, 0)),
            pl.BlockSpec((1, d), lambda i: (0, 0)),
            pl.BlockSpec((d, LANES), lambda i: (0, 0)),
        ],
        out_specs=[
            pl.BlockSpec((tm, d), lambda i: (i, 0)),
            pl.BlockSpec((tm, LANES), lambda i: (i, 0)),
        ],
        out_shape=[jax.ShapeDtypeStruct((t, d), BF16), jax.ShapeDtypeStruct((t, LANES), F32)],
        compiler_params=_cparams(("parallel",)),
        name="norm1",
    )(x2d, mod3, g.reshape(1, d), wdt)


def _inproj_kernel(h_ref, w_ref, o_ref):
    o_ref[...] = jnp.dot(h_ref[...], w_ref[...], preferred_element_type=F32)


def _inproj(h, w_bf, n_main):
    t, d = h.shape
    tn = 1152
    tm = min(2048, t)
    return pl.pallas_call(
        _inproj_kernel,
        grid=(n_main // tn, t // tm),
        in_specs=[
            pl.BlockSpec((tm, d), lambda j, i: (i, 0)),
            pl.BlockSpec((d, tn), lambda j, i: (0, j)),
        ],
        out_specs=pl.BlockSpec((tm, tn), lambda j, i: (i, j)),
        out_shape=jax.ShapeDtypeStruct((t, n_main), F32),
        compiler_params=_cparams(("parallel", "arbitrary")),
        name="inproj",
    )(h, w_bf)


def _lru_kernel(g_ref, x_ref, cw_ref, cb_ref, wab_ref, ba_ref, bi_ref, lam_ref, o_ref,
                xpad, a_scr, b_scr, hs, hcar):
    tt, c = x_ref.shape
    t = pl.program_id(2)

    @pl.when(t == 0)
    def _():
        xpad[0:SUBLANES, :] = jnp.zeros((SUBLANES, c), F32)
        hcar[...] = jnp.zeros_like(hcar)

    xpad[SUBLANES:SUBLANES + tt, :] = x_ref[...]
    y = _causal_conv(xpad, cw_ref, cb_ref, tt)
    xpad[0:SUBLANES, :] = xpad[tt:tt + SUBLANES, :]

    row = lax.broadcasted_iota(jnp.int32, (tt // SUBLANES, SUBLANES, LRU_HEAD_DIM), 1)
    for j in range(c // LRU_HEAD_DIM):
        sl = slice(j * LRU_HEAD_DIM, (j + 1) * LRU_HEAD_DIM)
        yj = y[:, sl]
        gm = jnp.dot(yj.astype(BF16), wab_ref[j], preferred_element_type=F32)
        r = jax.nn.sigmoid(gm[:, :LRU_HEAD_DIM] + ba_ref[:, sl])
        i = jax.nn.sigmoid(gm[:, LRU_HEAD_DIM:] + bi_ref[:, sl])
        log_a = (-LRU_C * _softplus(-lam_ref[:, sl])) * r
        a = jnp.exp(log_a)
        th = jnp.tanh(log_a)
        b = jnp.sqrt((-2.0 * th) / (1.0 - th)) * (i * yj)
        a = a.reshape(tt // SUBLANES, SUBLANES, LRU_HEAD_DIM)
        b = b.reshape(tt // SUBLANES, SUBLANES, LRU_HEAD_DIM)
        for s in (1, 2, 4):
            a_sh = pltpu.roll(a, s, 1)
            b_sh = pltpu.roll(b, s, 1)
            m = row >= s
            b = jnp.where(m, a * b_sh + b, b)
            a = jnp.where(m, a * a_sh, a)
        a_scr[:, sl] = a.reshape(tt, LRU_HEAD_DIM)
        b_scr[:, sl] = b.reshape(tt, LRU_HEAD_DIM)

    carry = hcar[...]
    for n in range(tt // SUBLANES):
        rs = slice(n * SUBLANES, (n + 1) * SUBLANES)
        hb = b_scr[rs, :] + a_scr[rs, :] * carry
        hs[rs, :] = hb
        carry = hb[SUBLANES - 1:SUBLANES, :]
    hcar[...] = carry
    o_ref[...] = (_gelu(g_ref[...]) * hs[...]).astype(BF16)


def _lru(proj, cw, cb, wab, ba, bi, lam, batch, seq, width, x_col0):
    t = proj.shape[0]
    c = 512
    tt = min(512, seq)
    nt = seq // tt
    ncb = width // c
    hpb = c // LRU_HEAD_DIM
    xoff = x_col0 // c
    return pl.pallas_call(
        _lru_kernel,
        grid=(batch, ncb, nt),
        in_specs=[
            pl.BlockSpec((tt, c), lambda b, j, i: (b * nt + i, j)),
            pl.BlockSpec((tt, c), lambda b, j, i: (b * nt + i, xoff + j)),
            pl.BlockSpec((CONV_WIDTH, c), lambda b, j, i: (0, j)),
            pl.BlockSpec((1, c), lambda b, j, i: (0, j)),
            pl.BlockSpec((hpb, LRU_HEAD_DIM, 2 * LRU_HEAD_DIM), lambda b, j, i: (j, 0, 0)),
            pl.BlockSpec((1, c), lambda b, j, i: (0, j)),
            pl.BlockSpec((1, c), lambda b, j, i: (0, j)),
            pl.BlockSpec((1, c), lambda b, j, i: (0, j)),
        ],
        out_specs=pl.BlockSpec((tt, c), lambda b, j, i: (b * nt + i, j)),
        out_shape=jax.ShapeDtypeStruct((t, width), BF16),
        scratch_shapes=[
            pltpu.VMEM((tt + SUBLANES, c), F32),
            pltpu.VMEM((tt, c), F32),
            pltpu.VMEM((tt, c), F32),
            pltpu.VMEM((tt, c), F32),
            pltpu.VMEM((1, c), F32),
        ],
        compiler_params=_cparams(("parallel", "parallel", "arbitrary")),
        name="lru",
    )(proj, proj, cw, cb, wab, ba, bi, lam)


def _ssd_kernel(z_ref, xbc_ref, dt_ref, cw_ref, cb_ref, dtb_ref, alog_ref, dsk_ref, ng_ref, e_ref, tri_ref,
                o_ref, xpad, st):
    L = SSD_CHUNK
    N = SSD_STATE
    w = z_ref.shape[1]
    gw = w // SSD_GROUPS
    gn = SSD_GROUPS * N

    @pl.when(pl.program_id(1) == 0)
    def _():
        xpad[0:SUBLANES, :] = jnp.zeros((SUBLANES, xpad.shape[1]), F32)
        st[...] = jnp.zeros_like(st)

    xpad[SUBLANES:SUBLANES + L, :] = xbc_ref[...]
    y = _causal_conv(xpad, cw_ref, cb_ref, L)
    xpad[0:SUBLANES, :] = xpad[L:L + SUBLANES, :]
    act = _silu(y)
    xs = act[:, :w]
    bm = act[:, w:w + gn]
    cm = act[:, w + gn:]

    dt = _softplus(dt_ref[...] + dtb_ref[...])
    da = dt * (-jnp.exp(alog_ref[...]))
    a_cs = _dot_sel_left(tri_ref[...], da)
    a_cs_t = a_cs.T
    ea = jnp.exp(a_cs)
    dte = jnp.exp(a_cs[L - 1:L, :] - a_cs)
    sel = e_ref[...]
    dt_f = _dot_sel_right(dt, sel)
    ea_f = _dot_sel_right(ea, sel)
    dte_f = _dot_sel_right(dte, sel)
    xc = xs * dt_f
    xcb = xc.astype(BF16)
    xd = (xc * dte_f).astype(BF16)

    causal = lax.broadcasted_iota(jnp.int32, (L, L), 0) >= lax.broadcasted_iota(jnp.int32, (L, L), 1)
    lane = lax.broadcasted_iota(jnp.int32, (L, LANES), 1)
    low_half = lane < SSD_HEAD_DIM
    heads_per_group = gw // SSD_HEAD_DIM
    for g in range(SSD_GROUPS):
        gs = slice(g * gw, (g + 1) * gw)
        cg = cm[:, g * N:(g + 1) * N].astype(BF16)
        bg = bm[:, g * N:(g + 1) * N]
        cbm = lax.dot_general(cg, bg.astype(BF16), (((1,), (1,)), ((), ())), preferred_element_type=F32)
        bg_t = bg.T.astype(BF16)
        s_prev = st[g]
        y_off = jnp.dot(cg, s_prev.astype(BF16), preferred_element_type=F32) * ea_f[:, gs]
        s_new = jnp.dot(bg_t, xd[:, gs], preferred_element_type=F32)
        st[g] = ea_f[L - 1:L, gs] * s_prev + s_new
        for p in range(heads_per_group // 2):
            h0 = g * heads_per_group + 2 * p
            cs = slice(h0 * SSD_HEAD_DIM, (h0 + 2) * SSD_HEAD_DIM)
            ms = []
            for hh in (h0, h0 + 1):
                seg = a_cs[:, hh:hh + 1] - a_cs_t[hh:hh + 1, :]
                ms.append(cbm * jnp.exp(jnp.where(causal, seg, NEG_BIG)))
            lhs = jnp.concatenate(ms, axis=1).astype(BF16)
            pair = xcb[:, cs]
            zero = jnp.zeros_like(pair)
            rhs = jnp.concatenate([jnp.where(low_half, pair, zero), jnp.where(low_half, zero, pair)], axis=0)
            y_diag = jnp.dot(lhs, rhs, preferred_element_type=F32)
            yp = y_diag + y_off[:, 2 * p * SSD_HEAD_DIM:(2 * p + 2) * SSD_HEAD_DIM] + dsk_ref[:, cs] * xs[:, cs]
            xpad[SUBLANES:SUBLANES + L, cs] = yp

    yz = xpad[SUBLANES:SUBLANES + L, 0:w] * _silu(z_ref[...])
    for g in range(SSD_GROUPS):
        gs = slice(g * gw, (g + 1) * gw)
        v = yz[:, gs]
        v = v * lax.rsqrt(jnp.mean(v * v, axis=-1, keepdims=True) + EPS)
        o_ref[:, gs] = (v * ng_ref[:, gs]).astype(BF16)


def _ssd(proj, dt_raw, cw, cb, dtb, alog, dsk, ng, sel, tri, batch, seq, width, z_col0, xbc_col0):
    t = proj.shape[0]
    xw = cw.shape[1]
    L = SSD_CHUNK
    nc = seq // L
    zoff = z_col0 // width
    xoff = xbc_col0 // xw
    return pl.pallas_call(
        _ssd_kernel,
        grid=(batch, nc),
        in_specs=[
            pl.BlockSpec((L, width), lambda b, i: (b * nc + i, zoff)),
            pl.BlockSpec((L, xw), lambda b, i: (b * nc + i, xoff)),
            pl.BlockSpec((L, LANES), lambda b, i: (b * nc + i, 0)),
            pl.BlockSpec((CONV_WIDTH, xw), lambda b, i: (0, 0)),
            pl.BlockSpec((1, xw), lambda b, i: (0, 0)),
            pl.BlockSpec((1, LANES), lambda b, i: (0, 0)),
            pl.BlockSpec((1, LANES), lambda b, i: (0, 0)),
            pl.BlockSpec((1, width), lambda b, i: (0, 0)),
            pl.BlockSpec((1, width), lambda b, i: (0, 0)),
            pl.BlockSpec((LANES, width), lambda b, i: (0, 0)),
            pl.BlockSpec((L, L), lambda b, i: (0, 0)),
        ],
        out_specs=pl.BlockSpec((L, width), lambda b, i: (b * nc + i, 0)),
        out_shape=jax.ShapeDtypeStruct((t, width), BF16),
        scratch_shapes=[
            pltpu.VMEM((L + SUBLANES, xw), F32),
            pltpu.VMEM((SSD_GROUPS, SSD_STATE, width // SSD_GROUPS), F32),
        ],
        compiler_params=_cparams(("parallel", "arbitrary")),
        name="ssd",
    )(proj, proj, dt_raw, cw, cb, dtb, alog, dsk, ng, sel, tri)


def _outproj_kernel(lru_ref, ssd_ref, wo_ref, x_ref, mod_ref, g_ref, wq_ref, x1_ref, h2t_ref, q_ref, acc):
    k = pl.program_id(1)
    nk = pl.num_programs(1)

    @pl.when(k == 0)
    def _():
        acc[...] = jnp.zeros_like(acc)

    @pl.when(k < nk // 2)
    def _():
        acc[...] += jnp.dot(lru_ref[...], wo_ref[...], preferred_element_type=F32)

    @pl.when(k >= nk // 2)
    def _():
        acc[...] += jnp.dot(ssd_ref[...], wo_ref[...], preferred_element_type=F32)

    @pl.when(k == nk - 1)
    def _():
        x1 = x_ref[...] + mod_ref[2:3, :] * acc[...]
        x1_ref[...] = x1
        y = x1 * lax.rsqrt(jnp.mean(x1 * x1, axis=-1, keepdims=True) + EPS) * g_ref[...]
        h2 = y * (1.0 + mod_ref[4:5, :]) + mod_ref[3:4, :]
        h2t_ref[...] = h2.T.astype(BF16)
        q_ref[...] = jnp.dot(h2.astype(BF16), wq_ref[...], preferred_element_type=F32)


def _outproj(lru, ssd, wo, x2d, mod3, g, wq, seq):
    t, d = x2d.shape
    half = lru.shape[1]
    tm = min(512, seq)
    tk = 1024
    nkh = half // tk
    per_batch = seq // tm
    nq = wq.shape[1]
    return pl.pallas_call(
        _outproj_kernel,
        grid=(t // tm, 2 * nkh),
        in_specs=[
            pl.BlockSpec((tm, tk), lambda i, k: (i, jnp.minimum(k, nkh - 1))),
            pl.BlockSpec((tm, tk), lambda i, k: (i, jnp.maximum(k - nkh, 0))),
            pl.BlockSpec((tk, d), lambda i, k: (k, 0)),
            pl.BlockSpec((tm, d), lambda i, k: (i, 0)),
            pl.BlockSpec((None, 6, d), lambda i, k: (i // per_batch, 0, 0)),
            pl.BlockSpec((1, d), lambda i, k: (0, 0)),
            pl.BlockSpec((d, nq), lambda i, k: (0, 0)),
        ],
        out_specs=[
            pl.BlockSpec((tm, d), lambda i, k: (i, 0)),
            pl.BlockSpec((d, tm), lambda i, k: (0, i)),
            pl.BlockSpec((tm, nq), lambda i, k: (i, 0)),
        ],
        out_shape=[
            jax.ShapeDtypeStruct((t, d), F32),
            jax.ShapeDtypeStruct((d, t), BF16),
            jax.ShapeDtypeStruct((t, nq), F32),
        ],
        scratch_shapes=[pltpu.VMEM((tm, d), F32)],
        compiler_params=_cparams(("parallel", "arbitrary")),
        name="outproj",
    )(lru, ssd, wo, x2d, mod3, g.reshape(1, d), wq)


def _sorting_network(n):
    pairs = []

    def merge(lo, hi, r):
        step = r * 2
        if step < hi - lo:
            merge(lo, hi, step)
            merge(lo + r, hi, step)
            pairs.extend((i, i + r) for i in range(lo + r, hi - r, step))
        else:
            pairs.append((lo, lo + r))

    def sort(lo, hi):
        if hi - lo >= 1:
            mid = lo + (hi - lo) // 2
            sort(lo, mid)
            sort(mid + 1, hi)
            merge(lo, hi, 1)

    sort(0, n - 1)
    return pairs


def _top_values(s, count):
    rows, tb = s.shape
    n = rows // SUBLANES
    assert n == count
    a = [s[k * SUBLANES:(k + 1) * SUBLANES, :] for k in range(n)]
    for i, j in _sorting_network(n):
        a[i], a[j] = jnp.maximum(a[i], a[j]), jnp.minimum(a[i], a[j])
    rid = lax.broadcasted_iota(jnp.int32, (count, tb), 0)
    out = jnp.zeros((count, tb), F32)
    for r in range(count):
        m = jnp.max(a[0], axis=0, keepdims=True)
        out = jnp.where(rid == r, m, out)
        hit = a[0] == m
        for k in range(count - 1 - r):
            a[k] = jnp.where(hit, a[k + 1], a[k])
    return out


def _pair_candidates(v1, v2):
    parts = [v1[0:1] + v2]
    parts += [v1[a:a + 1] + v2[0:8] for a in (1, 2, 3)]
    parts += [v1[0:8] + v2[b:b + 1] for b in (0, 1, 2)]
    parts.append(v1[8:16] + v2[0:1])
    return jnp.concatenate(parts, axis=0)


def _kth_and_expsum(cand, count):
    z = jnp.zeros((1, cand.shape[1]), F32)
    m = None
    for _ in range(count):
        m = jnp.max(cand, axis=0, keepdims=True)
        z = z + jnp.exp2(m)
        cand = jnp.where(cand == m, NEG_BIG, cand)
    return m, z


def _route_kernel(q_ref, keys_ref, s1_ref, s2_ref, tau_ref):
    for h in range(PEER_HEADS):
        ss = []
        for half in range(2):
            col = (2 * h + half) * LANES
            qh = q_ref[:, col:col + LANES].astype(BF16)
            s = lax.dot_general(keys_ref[h, half], qh, (((1,), (1,)), ((), ())), preferred_element_type=F32)
            ss.append(s - jnp.max(s, axis=0, keepdims=True))
        s1, s2 = ss
        v1 = _top_values(s1, PEER_TOPK)
        v2 = _top_values(s2, PEER_TOPK)
        _, z = _kth_and_expsum(_pair_candidates(v1, v2), PEER_TOPK)
        lz = jnp.log2(z)
        s1 = s1 - lz
        tau, _ = _kth_and_expsum(_pair_candidates(v1 - lz, v2), PEER_TOPK)
        s1_ref[h] = s1
        s2_ref[h] = s2
        tau_ref[h:h + 1, :] = tau


def _route(q, keys):
    t, nq = q.shape
    tb = min(512, t)
    nk = keys.shape[2]
    return pl.pallas_call(
        _route_kernel,
        grid=(t // tb,),
        in_specs=[
            pl.BlockSpec((tb, nq), lambda i: (i, 0)),
            pl.BlockSpec(keys.shape, lambda i: (0, 0, 0, 0)),
        ],
        out_specs=[
            pl.BlockSpec((PEER_HEADS, nk, tb), lambda i: (0, 0, i)),
            pl.BlockSpec((PEER_HEADS, nk, tb), lambda i: (0, 0, i)),
            pl.BlockSpec((PEER_HEADS, tb), lambda i: (0, i)),
        ],
        out_shape=[
            jax.ShapeDtypeStruct((PEER_HEADS, nk, t), F32),
            jax.ShapeDtypeStruct((PEER_HEADS, nk, t), F32),
            jax.ShapeDtypeStruct((PEER_HEADS, t), F32),
        ],
        compiler_params=_cparams(("parallel",)),
        name="route",
    )(q, keys)


def _vt_kernel(v_ref, o_ref):
    o_ref[...] = v_ref[...].T.astype(BF16)


def _transpose_bf16(v):
    ne, d = v.shape
    eb = 512
    return pl.pallas_call(
        _vt_kernel,
        grid=(ne // eb,),
        in_specs=[pl.BlockSpec((eb, d), lambda e: (e, 0))],
        out_specs=pl.BlockSpec((d, eb), lambda e: (0, e)),
        out_shape=jax.ShapeDtypeStruct((d, ne), BF16),
        compiler_params=_cparams(("parallel",)),
        name="vt",
    )(v)


def _peer_kernel(h2t_ref, u_ref, vt_ref, s1_ref, s2_ref, tau_ref, o_ref, at0, at1, w_scr, *, n_eblk):
    p = pl.program_id(0)
    ts = h2t_ref.shape[1]
    eb = u_ref.shape[0]
    e_prev = jnp.maximum(p - 1, 0) % n_eblk

    @pl.when(p == 0)
    def _():
        at1[...] = jnp.zeros_like(at1)

    @pl.when(e_prev == 0)
    def _():
        o_ref[...] = jnp.zeros_like(o_ref)

    tau = jnp.where(p > 0, tau_ref[...], -NEG_BIG)

    def step(at_new, at_old):
        at_new[...] = jnp.dot(u_ref[...].astype(BF16), h2t_ref[...], preferred_element_type=F32)
        for it in range(eb // PEER_KEYS):
            rows = slice(it * PEER_KEYS, (it + 1) * PEER_KEYS)
            for lc in range(ts // LANES):
                ls = slice(lc * LANES, (lc + 1) * LANES)
                wsum = None
                for h in range(PEER_HEADS):
                    tsum = s1_ref[h, it:it + 1, ls] + s2_ref[h, :, ls]
                    term = jnp.where(tsum >= tau[h:h + 1, ls], jnp.exp2(tsum), 0.0)
                    wsum = term if wsum is None else wsum + term
                w_scr[rows, ls] = wsum
        g = (_gelu(at_old[...]) * w_scr[...]).astype(BF16)
        o_ref[...] += jnp.dot(vt_ref[...], g, preferred_element_type=F32)

    @pl.when(p % 2 == 0)
    def _():
        step(at0, at1)

    @pl.when(p % 2 == 1)
    def _():
        step(at1, at0)


def _peer(h2t, u, vt, s1, s2, tau):
    d, t = h2t.shape
    ne = u.shape[0]
    ts = min(512, t)
    eb = 1024
    n_eblk = ne // eb
    n_log = (t // ts) * n_eblk

    def cur(p):
        n = jnp.minimum(p, n_log - 1)
        return n // n_eblk, n % n_eblk

    def prev(p):
        n = jnp.maximum(p - 1, 0)
        return n // n_eblk, n % n_eblk

    return pl.pallas_call(
        functools.partial(_peer_kernel, n_eblk=n_eblk),
        grid=(n_log + 1,),
        in_specs=[
            pl.BlockSpec((d, ts), lambda p: (0, cur(p)[0])),
            pl.BlockSpec((eb, d), lambda p: (cur(p)[1], 0)),
            pl.BlockSpec((d, eb), lambda p: (0, prev(p)[1])),
            pl.BlockSpec((PEER_HEADS, eb // PEER_KEYS, ts), lambda p: (0, prev(p)[1], prev(p)[0])),
            pl.BlockSpec((PEER_HEADS, PEER_KEYS, ts), lambda p: (0, 0, prev(p)[0])),
            pl.BlockSpec((PEER_HEADS, ts), lambda p: (0, prev(p)[0])),
        ],
        out_specs=pl.BlockSpec((d, ts), lambda p: (0, prev(p)[0])),
        out_shape=jax.ShapeDtypeStruct((d, t), F32),
        scratch_shapes=[pltpu.VMEM((eb, ts), F32), pltpu.VMEM((eb, ts), F32), pltpu.VMEM((eb, ts), F32)],
        compiler_params=_cparams(("arbitrary",)),
        name="peer",
    )(h2t, u, vt, s1, s2, tau)


def _final_kernel(x1_ref, pt_ref, mod_ref, g_ref, o_ref):
    x2 = x1_ref[...] + mod_ref[5:6, :] * pt_ref[...].T
    o_ref[...] = x2 * lax.rsqrt(jnp.mean(x2 * x2, axis=-1, keepdims=True) + EPS) * g_ref[...]


def _final(x1, peer_t, mod3, g, seq):
    t, d = x1.shape
    tm = min(512, seq)
    per_batch = seq // tm
    return pl.pallas_call(
        _final_kernel,
        grid=(t // tm,),
        in_specs=[
            pl.BlockSpec((tm, d), lambda i: (i, 0)),
            pl.BlockSpec((d, tm), lambda i: (0, i)),
            pl.BlockSpec((None, 6, d), lambda i: (i // per_batch, 0, 0)),
            pl.BlockSpec((1, d), lambda i: (0, 0)),
        ],
        out_specs=pl.BlockSpec((tm, d), lambda i: (i, 0)),
        out_shape=jax.ShapeDtypeStruct((t, d), F32),
        compiler_params=_cparams(("parallel",)),
        name="final",
    )(x1, peer_t, mod3, g.reshape(1, d))


def _layer(x2d, mod3, batch, seq, norm1_g, w_in, lru_conv_w, lru_conv_b, lru_w_a, lru_b_a, lru_w_i, lru_b_i,
           lru_lambda, ssd_conv_w, ssd_conv_b, ssd_dt_bias, ssd_a_log, ssd_d, ssd_norm_g, w_out, norm2_g,
           peer_w_q, peer_sub_keys, peer_u, peer_v):
    d = x2d.shape[1]
    lru_w = lru_lambda.shape[0]
    ssd_w = ssd_norm_g.shape[0]
    xbc_w = ssd_conv_w.shape[1]
    n_heads = ssd_dt_bias.shape[0]
    o_lru_x = lru_w
    o_ssd_z = 2 * lru_w
    o_ssd_xbc = o_ssd_z + ssd_w
    o_ssd_dt = o_ssd_xbc + xbc_w

    w_bf = w_in.astype(BF16)
    wdt = jnp.pad(w_bf[:, o_ssd_dt:o_ssd_dt + n_heads], ((0, 0), (0, LANES - n_heads)))
    wab = jnp.concatenate([lru_w_a, lru_w_i], axis=-1).astype(BF16)
    pad_h = lambda v: jnp.pad(v, (0, LANES - n_heads)).reshape(1, LANES)
    head_of_channel = jnp.arange(ssd_w, dtype=jnp.int32) // SSD_HEAD_DIM
    sel = (jnp.arange(LANES, dtype=jnp.int32)[:, None] == head_of_channel[None, :]).astype(BF16)
    tri = (jnp.arange(SSD_CHUNK)[:, None] >= jnp.arange(SSD_CHUNK)[None, :]).astype(BF16)
    dsk = jnp.repeat(ssd_d, SSD_HEAD_DIM).reshape(1, ssd_w)

    h, dt_raw = _norm1(x2d, mod3, norm1_g, wdt, seq)
    proj = _inproj(h, w_bf, o_ssd_dt)
    lru = _lru(proj, lru_conv_w, lru_conv_b.reshape(1, lru_w), wab, lru_b_a.reshape(1, lru_w),
               lru_b_i.reshape(1, lru_w), lru_lambda.reshape(1, lru_w), batch, seq, lru_w, o_lru_x)
    ssd = _ssd(proj, dt_raw, ssd_conv_w, ssd_conv_b.reshape(1, xbc_w), pad_h(ssd_dt_bias), pad_h(ssd_a_log),
               dsk, ssd_norm_g.reshape(1, ssd_w), sel, tri, batch, seq, ssd_w, o_ssd_z, o_ssd_xbc)
    x1, h2t, q = _outproj(lru, ssd, w_out.astype(BF16), x2d, mod3, norm2_g, peer_w_q.astype(BF16), seq)
    s1, s2, tau = _route(q, (peer_sub_keys * math.log2(math.e)).astype(BF16))
    peer_t = _peer(h2t, peer_u, _transpose_bf16(peer_v), s1, s2, tau)
    return x1, peer_t


def kernel(x, c, w_ada, b_ada, norm1_g, w_in, lru_conv_w, lru_conv_b, lru_w_a, lru_b_a, lru_w_i, lru_b_i, lru_lambda, ssd_conv_w, ssd_conv_b, ssd_dt_bias, ssd_a_log, ssd_d, ssd_norm_g, w_out, norm2_g, peer_w_q, peer_sub_keys, peer_u, peer_v, final_norm_g):
    batch, seq, d = x.shape
    depth = w_ada.shape[0]
    assert depth == 1, "the final residual is fused with the last layer; only depth 1 is wired up"
    x2d = x.reshape(batch * seq, d)
    l = 0
    mod3 = _ada(c, w_ada[l], b_ada[l]).reshape(batch, 6, d)
    x1, peer_t = _layer(x2d, mod3, batch, seq, norm1_g[l], w_in[l], lru_conv_w[l], lru_conv_b[l], lru_w_a[l],
                        lru_b_a[l], lru_w_i[l], lru_b_i[l], lru_lambda[l], ssd_conv_w[l], ssd_conv_b[l],
                        ssd_dt_bias[l], ssd_a_log[l], ssd_d[l], ssd_norm_g[l], w_out[l], norm2_g[l],
                        peer_w_q[l], peer_sub_keys[l], peer_u[l], peer_v[l])
    y = _final(x1, peer_t, mod3, final_norm_g, seq)
    return y.reshape(batch, seq, d)
```

```python
import functools
import math

import jax
import jax.numpy as jnp
from jax import lax
from jax.experimental import pallas as pl
from jax.experimental.pallas import tpu as pltpu

F32 = jnp.float32
BF16 = jnp.bfloat16

EPS = 1e-6
CONV_WIDTH = 4
LRU_C = 8.0
LRU_HEAD_DIM = 128
SSD_HEAD_DIM = 64
SSD_GROUPS = 4
SSD_STATE = 128
SSD_CHUNK = 128
PEER_HEADS = 8
PEER_KEYS = 128
PEER_TOPK = 16
LANES = 128
SUBLANES = 8
NEG_BIG = -1e30
VMEM_LIMIT = 56 * 1024 * 1024


def _cparams(sem, flags=None):
    return pltpu.CompilerParams(dimension_semantics=sem, vmem_limit_bytes=VMEM_LIMIT, flags=flags)


def _split3(x):
    hi = x.astype(BF16)
    r1 = x - hi.astype(F32)
    mid = r1.astype(BF16)
    lo = (r1 - mid.astype(F32)).astype(BF16)
    return hi, mid, lo


def _dot_sel_right(x, sel):
    return sum(jnp.dot(p, sel, preferred_element_type=F32) for p in _split3(x))


def _dot_sel_left(sel, x):
    return sum(jnp.dot(sel, p, preferred_element_type=F32) for p in _split3(x))


def _softplus(x):
    return jnp.maximum(x, 0.0) + jnp.log1p(jnp.exp(-jnp.abs(x)))


def _silu(x):
    return x * jax.nn.sigmoid(x)


def _causal_conv(xpad, cw_ref, cb_ref, rows):
    e = xpad[0:SUBLANES + rows, :]
    z = cw_ref[0:1, :] * e
    for k in range(1, CONV_WIDTH):
        z = pltpu.roll(z, 1, 0) + cw_ref[k:k + 1, :] * e
    return z[SUBLANES:, :] + cb_ref[...]


def _gelu(x):
    c = math.sqrt(2.0 / math.pi)
    hx = 0.5 * x
    return hx + hx * jnp.tanh(x * (c + (c * 0.044715) * (x * x)))


def _ada_kernel(c_ref, w_ref, b_ref, o_ref, act_ref):
    nb = c_ref.shape[0]
    tn = w_ref.shape[1]

    @pl.when(pl.program_id(0) == 0)
    def _():
        cv = c_ref[...]
        act_ref[...] = _silu(cv)

    for k in range(tn // LANES):
        sl = slice(k * LANES, (k + 1) * LANES)
        wk = w_ref[:, sl]
        for b in range(nb):
            o_ref[b:b + 1, sl] = jnp.sum(wk * act_ref[b], axis=0, keepdims=True) + b_ref[:, sl]


def _ada(c, w_ada, b_ada):
    nb, d = c.shape
    n = w_ada.shape[1]
    tn = 1024
    cb = jnp.broadcast_to(c[:, :, None], (nb, d, LANES))
    return pl.pallas_call(
        _ada_kernel,
        grid=(n // tn,),
        in_specs=[
            pl.BlockSpec((nb, d, LANES), lambda j: (0, 0, 0)),
            pl.BlockSpec((d, tn), lambda j: (0, j)),
            pl.BlockSpec((1, tn), lambda j: (0, j)),
        ],
        out_specs=pl.BlockSpec((nb, tn), lambda j: (0, j)),
        out_shape=jax.ShapeDtypeStruct((nb, n), F32),
        scratch_shapes=[pltpu.VMEM((nb, d, LANES), F32)],
        compiler_params=_cparams(("arbitrary",)),
        name="ada",
    )(cb, w_ada, b_ada.reshape(1, n))


def _norm1_kernel(x_ref, mod_ref, g_ref, wdt_ref, h_ref, dt_ref):
    x = x_ref[...]
    y = x * lax.rsqrt(jnp.mean(x * x, axis=-1, keepdims=True) + EPS) * g_ref[...]
    h = (y * (1.0 + mod_ref[1:2, :]) + mod_ref[0:1, :]).astype(BF16)
    h_ref[...] = h
    dt_ref[...] = jnp.dot(h, wdt_ref[...], preferred_element_type=F32)


def _norm1(x2d, mod3, g, wdt, seq):
    t, d = x2d.shape
    tm = min(512, seq)
    per_batch = seq // tm
    return pl.pallas_call(
        _norm1_kernel,
        grid=(t // tm,),
        in_specs=[
            pl.BlockSpec((tm, d), lambda i: (i, 0)),
            pl.BlockSpec((None, 6, d), lambda i: (i // per_batch, 0, 0)),
            pl.BlockSpec((1, d), lambda i: (0, 0)),
            pl.BlockSpec((d, LANES), lambda i: (0, 0)),
        ],
        out_specs=[
            pl.BlockSpec((tm, d), lambda i: (i, 0)),
            pl.BlockSpec((tm, LANES), lambda i: (i, 0)),
        ],
        out_shape=[jax.ShapeDtypeStruct((t, d), BF16), jax.ShapeDtypeStruct((t, LANES), F32)],
        compiler_params=_cparams(("parallel",)),
        name="norm1",
    )(x2d, mod3, g.reshape(1, d), wdt)


def _inproj_kernel(h_ref, w_ref, o_ref):
    o_ref[...] = jnp.dot(h_ref[...], w_ref[...], preferred_element_type=F32)


def _inproj(h, w_bf, n_main):
    t, d = h.shape
    tn = 2304
    tm = min(1024, t)
    return pl.pallas_call(
        _inproj_kernel,
        grid=(n_main // tn, t // tm),
        in_specs=[
            pl.BlockSpec((tm, d), lambda j, i: (i, 0)),
            pl.BlockSpec((d, tn), lambda j, i: (0, j)),
        ],
        out_specs=pl.BlockSpec((tm, tn), lambda j, i: (i, j)),
        out_shape=jax.ShapeDtypeStruct((t, n_main), F32),
        compiler_params=_cparams(("parallel", "arbitrary")),
        name="inproj",
    )(h, w_bf)


def _lru_kernel(g_ref, x_ref, cw_ref, cb_ref, wab_ref, ba_ref, bi_ref, lam_ref, o_ref,
                xpad, a_scr, b_scr, hs, hcar):
    tt, c = x_ref.shape
    t = pl.program_id(2)

    @pl.when(t == 0)
    def _():
        xpad[0:SUBLANES, :] = jnp.zeros((SUBLANES, c), F32)
        hcar[...] = jnp.zeros_like(hcar)

    xpad[SUBLANES:SUBLANES + tt, :] = x_ref[...]
    y = _causal_conv(xpad, cw_ref, cb_ref, tt)
    xpad[0:SUBLANES, :] = xpad[tt:tt + SUBLANES, :]

    row = lax.broadcasted_iota(jnp.int32, (tt // SUBLANES, SUBLANES, LRU_HEAD_DIM), 1)
    for j in range(c // LRU_HEAD_DIM):
        sl = slice(j * LRU_HEAD_DIM, (j + 1) * LRU_HEAD_DIM)
        yj = y[:, sl]
        gm = jnp.dot(yj.astype(BF16), wab_ref[j], preferred_element_type=F32)
        r = jax.nn.sigmoid(gm[:, :LRU_HEAD_DIM] + ba_ref[:, sl])
        i = jax.nn.sigmoid(gm[:, LRU_HEAD_DIM:] + bi_ref[:, sl])
        log_a = (-LRU_C * _softplus(-lam_ref[:, sl])) * r
        a = jnp.exp(log_a)
        th = jnp.tanh(log_a)
        b = jnp.sqrt((-2.0 * th) / (1.0 - th)) * (i * yj)
        a = a.reshape(tt // SUBLANES, SUBLANES, LRU_HEAD_DIM)
        b = b.reshape(tt // SUBLANES, SUBLANES, LRU_HEAD_DIM)
        for s in (1, 2, 4):
            a_sh = pltpu.roll(a, s, 1)
            b_sh = pltpu.roll(b, s, 1)
            m = row >= s
            b = jnp.where(m, a * b_sh + b, b)
            a = jnp.where(m, a * a_sh, a)
        a_scr[:, sl] = a.reshape(tt, LRU_HEAD_DIM)
        b_scr[:, sl] = b.reshape(tt, LRU_HEAD_DIM)

    carry = hcar[...]
    for n in range(tt // SUBLANES):
        rs = slice(n * SUBLANES, (n + 1) * SUBLANES)
        hb = b_scr[rs, :] + a_scr[rs, :] * carry
        hs[rs, :] = hb
        carry = hb[SUBLANES - 1:SUBLANES, :]
    hcar[...] = carry
    o_ref[...] = (_gelu(g_ref[...]) * hs[...]).astype(BF16)


def _lru(proj, cw, cb, wab, ba, bi, lam, batch, seq, width, x_col0):
    t = proj.shape[0]
    c = 512
    tt = min(512, seq)
    nt = seq // tt
    ncb = width // c
    hpb = c // LRU_HEAD_DIM
    xoff = x_col0 // c
    return pl.pallas_call(
        _lru_kernel,
        grid=(batch, ncb, nt),
        in_specs=[
            pl.BlockSpec((tt, c), lambda b, j, i: (b * nt + i, j)),
            pl.BlockSpec((tt, c), lambda b, j, i: (b * nt + i, xoff + j)),
            pl.BlockSpec((CONV_WIDTH, c), lambda b, j, i: (0, j)),
            pl.BlockSpec((1, c), lambda b, j, i: (0, j)),
            pl.BlockSpec((hpb, LRU_HEAD_DIM, 2 * LRU_HEAD_DIM), lambda b, j, i: (j, 0, 0)),
            pl.BlockSpec((1, c), lambda b, j, i: (0, j)),
            pl.BlockSpec((1, c), lambda b, j, i: (0, j)),
            pl.BlockSpec((1, c), lambda b, j, i: (0, j)),
        ],
        out_specs=pl.BlockSpec((tt, c), lambda b, j, i: (b * nt + i, j)),
        out_shape=jax.ShapeDtypeStruct((t, width), BF16),
        scratch_shapes=[
            pltpu.VMEM((tt + SUBLANES, c), F32),
            pltpu.VMEM((tt, c), F32),
            pltpu.VMEM((tt, c), F32),
            pltpu.VMEM((tt, c), F32),
            pltpu.VMEM((1, c), F32),
        ],
        compiler_params=_cparams(("parallel", "parallel", "arbitrary")),
        name="lru",
    )(proj, proj, cw, cb, wab, ba, bi, lam)


def _ssd_kernel(z_ref, xbc_ref, dt_ref, cw_ref, cb_ref, dtb_ref, alog_ref, dsk_ref, ng_ref, e_ref, tri_ref,
                o_ref, xpad, st):
    L = SSD_CHUNK
    N = SSD_STATE
    w = z_ref.shape[1]
    gw = w // SSD_GROUPS
    gn = SSD_GROUPS * N

    @pl.when(pl.program_id(1) == 0)
    def _():
        xpad[0:SUBLANES, :] = jnp.zeros((SUBLANES, xpad.shape[1]), F32)
        st[...] = jnp.zeros_like(st)

    xpad[SUBLANES:SUBLANES + L, :] = xbc_ref[...]
    y = _causal_conv(xpad, cw_ref, cb_ref, L)
    xpad[0:SUBLANES, :] = xpad[L:L + SUBLANES, :]
    act = _silu(y)
    xs = act[:, :w]
    bm = act[:, w:w + gn]
    cm = act[:, w + gn:]

    dt = _softplus(dt_ref[...] + dtb_ref[...])
    da = dt * (-jnp.exp(alog_ref[...]))
    a_cs = _dot_sel_left(tri_ref[...], da)
    a_cs_t = a_cs.T
    ea = jnp.exp(a_cs)
    dte = jnp.exp(a_cs[L - 1:L, :] - a_cs)
    sel = e_ref[...]
    dt_f = _dot_sel_right(dt, sel)
    ea_f = _dot_sel_right(ea, sel)
    dte_f = _dot_sel_right(dte, sel)
    xc = xs * dt_f
    xcb = xc.astype(BF16)
    xd = (xc * dte_f).astype(BF16)

    causal = lax.broadcasted_iota(jnp.int32, (L, L), 0) >= lax.broadcasted_iota(jnp.int32, (L, L), 1)
    lane = lax.broadcasted_iota(jnp.int32, (L, LANES), 1)
    low_half = lane < SSD_HEAD_DIM
    heads_per_group = gw // SSD_HEAD_DIM
    for g in range(SSD_GROUPS):
        gs = slice(g * gw, (g + 1) * gw)
        cg = cm[:, g * N:(g + 1) * N].astype(BF16)
        bg = bm[:, g * N:(g + 1) * N]
        cbm = lax.dot_general(cg, bg.astype(BF16), (((1,), (1,)), ((), ())), preferred_element_type=F32)
        bg_t = bg.T.astype(BF16)
        s_prev = st[g]
        y_off = jnp.dot(cg, s_prev.astype(BF16), preferred_element_type=F32) * ea_f[:, gs]
        s_new = jnp.dot(bg_t, xd[:, gs], preferred_element_type=F32)
        st[g] = ea_f[L - 1:L, gs] * s_prev + s_new
        for p in range(heads_per_group // 2):
            h0 = g * heads_per_group + 2 * p
            cs = slice(h0 * SSD_HEAD_DIM, (h0 + 2) * SSD_HEAD_DIM)
            ms = []
            for hh in (h0, h0 + 1):
                seg = a_cs[:, hh:hh + 1] - a_cs_t[hh:hh + 1, :]
                ms.append(cbm * jnp.exp(jnp.where(causal, seg, NEG_BIG)))
            lhs = jnp.concatenate(ms, axis=1).astype(BF16)
            pair = xcb[:, cs]
            zero = jnp.zeros_like(pair)
            rhs = jnp.concatenate([jnp.where(low_half, pair, zero), jnp.where(low_half, zero, pair)], axis=0)
            y_diag = jnp.dot(lhs, rhs, preferred_element_type=F32)
            yp = y_diag + y_off[:, 2 * p * SSD_HEAD_DIM:(2 * p + 2) * SSD_HEAD_DIM] + dsk_ref[:, cs] * xs[:, cs]
            xpad[SUBLANES:SUBLANES + L, cs] = yp

    yz = xpad[SUBLANES:SUBLANES + L, 0:w] * _silu(z_ref[...])
    for g in range(SSD_GROUPS):
        gs = slice(g * gw, (g + 1) * gw)
        v = yz[:, gs]
        v = v * lax.rsqrt(jnp.mean(v * v, axis=-1, keepdims=True) + EPS)
        o_ref[:, gs] = (v * ng_ref[:, gs]).astype(BF16)


def _ssd(proj, dt_raw, cw, cb, dtb, alog, dsk, ng, sel, tri, batch, seq, width, z_col0, xbc_col0):
    t = proj.shape[0]
    xw = cw.shape[1]
    L = SSD_CHUNK
    nc = seq // L
    zoff = z_col0 // width
    xoff = xbc_col0 // xw
    return pl.pallas_call(
        _ssd_kernel,
        grid=(batch, nc),
        in_specs=[
            pl.BlockSpec((L, width), lambda b, i: (b * nc + i, zoff)),
            pl.BlockSpec((L, xw), lambda b, i: (b * nc + i, xoff)),
            pl.BlockSpec((L, LANES), lambda b, i: (b * nc + i, 0)),
            pl.BlockSpec((CONV_WIDTH, xw), lambda b, i: (0, 0)),
            pl.BlockSpec((1, xw), lambda b, i: (0, 0)),
            pl.BlockSpec((1, LANES), lambda b, i: (0, 0)),
            pl.BlockSpec((1, LANES), lambda b, i: (0, 0)),
            pl.BlockSpec((1, width), lambda b, i: (0, 0)),
            pl.BlockSpec((1, width), lambda b, i: (0, 0)),
            pl.BlockSpec((LANES, width), lambda b, i: (0, 0)),
            pl.BlockSpec((L, L), lambda b, i: (0, 0)),
        ],
        out_specs=pl.BlockSpec((L, width), lambda b, i: (b * nc + i, 0)),
        out_shape=jax.ShapeDtypeStruct((t, width), BF16),
        scratch_shapes=[
            pltpu.VMEM((L + SUBLANES, xw), F32),
            pltpu.VMEM((SSD_GROUPS, SSD_STATE, width // SSD_GROUPS), F32),
        ],
        compiler_params=_cparams(("parallel", "arbitrary")),
        name="ssd",
    )(proj, proj, dt_raw, cw, cb, dtb, alog, dsk, ng, sel, tri)


def _outproj_kernel(lru_ref, ssd_ref, wo_ref, x_ref, mod_ref, g_ref, wq_ref, x1_ref, h2t_ref, q_ref, acc):
    k = pl.program_id(1)
    nk = pl.num_programs(1)

    @pl.when(k == 0)
    def _():
        acc[...] = jnp.zeros_like(acc)

    @pl.when(k < nk // 2)
    def _():
        acc[...] += jnp.dot(lru_ref[...], wo_ref[...], preferred_element_type=F32)

    @pl.when(k >= nk // 2)
    def _():
        acc[...] += jnp.dot(ssd_ref[...], wo_ref[...], preferred_element_type=F32)

    @pl.when(k == nk - 1)
    def _():
        x1 = x_ref[...] + mod_ref[2:3, :] * acc[...]
        x1_ref[...] = x1
        y = x1 * lax.rsqrt(jnp.mean(x1 * x1, axis=-1, keepdims=True) + EPS) * g_ref[...]
        h2 = y * (1.0 + mod_ref[4:5, :]) + mod_ref[3:4, :]
        h2t_ref[...] = h2.T.astype(BF16)
        q_ref[...] = jnp.dot(h2.astype(BF16), wq_ref[...], preferred_element_type=F32)


def _outproj(lru, ssd, wo, x2d, mod3, g, wq, seq):
    t, d = x2d.shape
    half = lru.shape[1]
    tm = min(512, seq)
    tk = 1024
    nkh = half // tk
    per_batch = seq // tm
    nq = wq.shape[1]
    return pl.pallas_call(
        _outproj_kernel,
        grid=(t // tm, 2 * nkh),
        in_specs=[
            pl.BlockSpec((tm, tk), lambda i, k: (i, jnp.minimum(k, nkh - 1))),
            pl.BlockSpec((tm, tk), lambda i, k: (i, jnp.maximum(k - nkh, 0))),
            pl.BlockSpec((tk, d), lambda i, k: (k, 0)),
            pl.BlockSpec((tm, d), lambda i, k: (i, 0)),
            pl.BlockSpec((None, 6, d), lambda i, k: (i // per_batch, 0, 0)),
            pl.BlockSpec((1, d), lambda i, k: (0, 0)),
            pl.BlockSpec((d, nq), lambda i, k: (0, 0)),
        ],
        out_specs=[
            pl.BlockSpec((tm, d), lambda i, k: (i, 0)),
            pl.BlockSpec((d, tm), lambda i, k: (0, i)),
            pl.BlockSpec((tm, nq), lambda i, k: (i, 0)),
        ],
        out_shape=[
            jax.ShapeDtypeStruct((t, d), F32),
            jax.ShapeDtypeStruct((d, t), BF16),
            jax.ShapeDtypeStruct((t, nq), F32),
        ],
        scratch_shapes=[pltpu.VMEM((tm, d), F32)],
        compiler_params=_cparams(("parallel", "arbitrary")),
        name="outproj",
    )(lru, ssd, wo, x2d, mod3, g.reshape(1, d), wq)


def _sorting_network(n):
    pairs = []

    def merge(lo, hi, r):
        step = r * 2
        if step < hi - lo:
            merge(lo, hi, step)
            merge(lo + r, hi, step)
            pairs.extend((i, i + r) for i in range(lo + r, hi - r, step))
        else:
            pairs.append((lo, lo + r))

    def sort(lo, hi):
        if hi - lo >= 1:
            mid = lo + (hi - lo) // 2
            sort(lo, mid)
            sort(mid + 1, hi)
            merge(lo, hi, 1)

    sort(0, n - 1)
    return pairs


def _top_values(s, count):
    rows, tb = s.shape
    n = rows // SUBLANES
    assert n == count
    a = [s[k * SUBLANES:(k + 1) * SUBLANES, :] for k in range(n)]
    for i, j in _sorting_network(n):
        a[i], a[j] = jnp.maximum(a[i], a[j]), jnp.minimum(a[i], a[j])
    rid = lax.broadcasted_iota(jnp.int32, (count, tb), 0)
    out = jnp.zeros((count, tb), F32)
    for r in range(count):
        m = jnp.max(a[0], axis=0, keepdims=True)
        out = jnp.where(rid == r, m, out)
        hit = a[0] == m
        for k in range(count - 1 - r):
            a[k] = jnp.where(hit, a[k + 1], a[k])
    return out


def _pair_candidates(v1, v2):
    parts = [v1[0:1] + v2]
    parts += [v1[a:a + 1] + v2[0:8] for a in (1, 2, 3)]
    parts += [v1[0:8] + v2[b:b + 1] for b in (0, 1, 2)]
    parts.append(v1[8:16] + v2[0:1])
    return jnp.concatenate(parts, axis=0)


def _kth_and_expsum(cand, count):
    z = jnp.zeros((1, cand.shape[1]), F32)
    m = None
    for _ in range(count):
        m = jnp.max(cand, axis=0, keepdims=True)
        z = z + jnp.exp2(m)
        cand = jnp.where(cand == m, NEG_BIG, cand)
    return m, z


def _route_kernel(q_ref, keys_ref, s1_ref, s2_ref, tau_ref):
    for h in range(PEER_HEADS):
        ss = []
        for half in range(2):
            col = (2 * h + half) * LANES
            qh = q_ref[:, col:col + LANES].astype(BF16)
            s = lax.dot_general(keys_ref[h, half], qh, (((1,), (1,)), ((), ())), preferred_element_type=F32)
            ss.append(s - jnp.max(s, axis=0, keepdims=True))
        s1, s2 = ss
        v1 = _top_values(s1, PEER_TOPK)
        v2 = _top_values(s2, PEER_TOPK)
        _, z = _kth_and_expsum(_pair_candidates(v1, v2), PEER_TOPK)
        lz = jnp.log2(z)
        s1 = s1 - lz
        tau, _ = _kth_and_expsum(_pair_candidates(v1 - lz, v2), PEER_TOPK)
        s1_ref[h] = s1
        s2_ref[h] = s2
        tau_ref[h:h + 1, :] = tau


def _route(q, keys):
    t, nq = q.shape
    tb = min(512, t)
    nk = keys.shape[2]
    return pl.pallas_call(
        _route_kernel,
        grid=(t // tb,),
        in_specs=[
            pl.BlockSpec((tb, nq), lambda i: (i, 0)),
            pl.BlockSpec(keys.shape, lambda i: (0, 0, 0, 0)),
        ],
        out_specs=[
            pl.BlockSpec((PEER_HEADS, nk, tb), lambda i: (0, 0, i)),
            pl.BlockSpec((PEER_HEADS, nk, tb), lambda i: (0, 0, i)),
            pl.BlockSpec((PEER_HEADS, tb), lambda i: (0, i)),
        ],
        out_shape=[
            jax.ShapeDtypeStruct((PEER_HEADS, nk, t), F32),
            jax.ShapeDtypeStruct((PEER_HEADS, nk, t), F32),
            jax.ShapeDtypeStruct((PEER_HEADS, t), F32),
        ],
        compiler_params=_cparams(("parallel",)),
        name="route",
    )(q, keys)


def _vt_kernel(v_ref, o_ref):
    o_ref[...] = v_ref[...].T.astype(BF16)


def _transpose_bf16(v):
    ne, d = v.shape
    eb = 512
    return pl.pallas_call(
        _vt_kernel,
        grid=(ne // eb,),
        in_specs=[pl.BlockSpec((eb, d), lambda e: (e, 0))],
        out_specs=pl.BlockSpec((d, eb), lambda e: (0, e)),
        out_shape=jax.ShapeDtypeStruct((d, ne), BF16),
        compiler_params=_cparams(("parallel",)),
        name="vt",
    )(v)


def _peer_kernel(h2t_ref, u_ref, vt_ref, s1_ref, s2_ref, tau_ref, o_ref, at_scr, w_scr, *, tc):
    e = pl.program_id(1)
    ts = h2t_ref.shape[1]
    eb = u_ref.shape[0]

    @pl.when(e == 0)
    def _():
        o_ref[...] = jnp.zeros_like(o_ref)

    for c in range(ts // tc):
        cs = slice(c * tc, (c + 1) * tc)
        for it in range(eb // PEER_KEYS):
            rows = slice(it * PEER_KEYS, (it + 1) * PEER_KEYS)
            for lc in range(tc // LANES):
                ls = slice(c * tc + lc * LANES, c * tc + (lc + 1) * LANES)
                wsum = None
                for h in range(PEER_HEADS):
                    tsum = s1_ref[h, it:it + 1, ls] + s2_ref[h, :, ls]
                    term = jnp.where(tsum >= tau_ref[h:h + 1, ls], jnp.exp2(tsum), 0.0)
                    wsum = term if wsum is None else wsum + term
                w_scr[rows, lc * LANES:(lc + 1) * LANES] = wsum
        at_scr[...] = jnp.dot(u_ref[...].astype(BF16), h2t_ref[:, cs], preferred_element_type=F32)
        g = (_gelu(at_scr[...]) * w_scr[...]).astype(BF16)
        o_ref[:, cs] += jnp.dot(vt_ref[...], g, preferred_element_type=F32)


def _peer(h2t, u, vt, s1, s2, tau):
    d, t = h2t.shape
    ne = u.shape[0]
    ts = min(512, t)
    tc = min(512, ts)
    eb = 1024
    return pl.pallas_call(
        functools.partial(_peer_kernel, tc=tc),
        grid=(t // ts, ne // eb),
        in_specs=[
            pl.BlockSpec((d, ts), lambda s, e: (0, s)),
            pl.BlockSpec((eb, d), lambda s, e: (e, 0)),
            pl.BlockSpec((d, eb), lambda s, e: (0, e)),
            pl.BlockSpec((PEER_HEADS, eb // PEER_KEYS, ts), lambda s, e: (0, e, s)),
            pl.BlockSpec((PEER_HEADS, PEER_KEYS, ts), lambda s, e: (0, 0, s)),
            pl.BlockSpec((PEER_HEADS, ts), lambda s, e: (0, s)),
        ],
        out_specs=pl.BlockSpec((d, ts), lambda s, e: (0, s)),
        out_shape=jax.ShapeDtypeStruct((d, t), F32),
        scratch_shapes=[pltpu.VMEM((eb, tc), F32), pltpu.VMEM((eb, tc), F32)],
        compiler_params=_cparams(("parallel", "arbitrary")),
        name="peer",
    )(h2t, u, vt, s1, s2, tau)


def _final_kernel(x1_ref, pt_ref, mod_ref, g_ref, o_ref):
    x2 = x1_ref[...] + mod_ref[5:6, :] * pt_ref[...].T
    o_ref[...] = x2 * lax.rsqrt(jnp.mean(x2 * x2, axis=-1, keepdims=True) + EPS) * g_ref[...]


def _final(x1, peer_t, mod3, g, seq):
    t, d = x1.shape
    tm = min(512, seq)
    per_batch = seq // tm
    return pl.pallas_call(
        _final_kernel,
        grid=(t // tm,),
        in_specs=[
            pl.BlockSpec((tm, d), lambda i: (i, 0)),
            pl.BlockSpec((d, tm), lambda i: (0, i)),
            pl.BlockSpec((None, 6, d), lambda i: (i // per_batch, 0, 0)),
            pl.BlockSpec((1, d), lambda i: (0, 0)),
        ],
        out_specs=pl.BlockSpec((tm, d), lambda i: (i, 0)),
        out_shape=jax.ShapeDtypeStruct((t, d), F32),
        compiler_params=_cparams(("parallel",)),
        name="final",
    )(x1, peer_t, mod3, g.reshape(1, d))


def _layer(x2d, mod3, batch, seq, norm1_g, w_in, lru_conv_w, lru_conv_b, lru_w_a, lru_b_a, lru_w_i, lru_b_i,
           lru_lambda, ssd_conv_w, ssd_conv_b, ssd_dt_bias, ssd_a_log, ssd_d, ssd_norm_g, w_out, norm2_g,
           peer_w_q, peer_sub_keys, peer_u, peer_v):
    d = x2d.shape[1]
    lru_w = lru_lambda.shape[0]
    ssd_w = ssd_norm_g.shape[0]
    xbc_w = ssd_conv_w.shape[1]
    n_heads = ssd_dt_bias.shape[0]
    o_lru_x = lru_w
    o_ssd_z = 2 * lru_w
    o_ssd_xbc = o_ssd_z + ssd_w
    o_ssd_dt = o_ssd_xbc + xbc_w

    w_bf = w_in.astype(BF16)
    wdt = jnp.pad(w_bf[:, o_ssd_dt:o_ssd_dt + n_heads], ((0, 0), (0, LANES - n_heads)))
    wab = jnp.concatenate([lru_w_a, lru_w_i], axis=-1).astype(BF16)
    pad_h = lambda v: jnp.pad(v, (0, LANES - n_heads)).reshape(1, LANES)
    head_of_channel = jnp.arange(ssd_w, dtype=jnp.int32) // SSD_HEAD_DIM
    sel = (jnp.arange(LANES, dtype=jnp.int32)[:, None] == head_of_channel[None, :]).astype(BF16)
    tri = (jnp.arange(SSD_CHUNK)[:, None] >= jnp.arange(SSD_CHUNK)[None, :]).astype(BF16)
    dsk = jnp.repeat(ssd_d, SSD_HEAD_DIM).reshape(1, ssd_w)

    h, dt_raw = _norm1(x2d, mod3, norm1_g, wdt, seq)
    proj = _inproj(h, w_bf, o_ssd_dt)
    lru = _lru(proj, lru_conv_w, lru_conv_b.reshape(1, lru_w), wab, lru_b_a.reshape(1, lru_w),
               lru_b_i.reshape(1, lru_w), lru_lambda.reshape(1, lru_w), batch, seq, lru_w, o_lru_x)
    ssd = _ssd(proj, dt_raw, ssd_conv_w, ssd_conv_b.reshape(1, xbc_w), pad_h(ssd_dt_bias), pad_h(ssd_a_log),
               dsk, ssd_norm_g.reshape(1, ssd_w), sel, tri, batch, seq, ssd_w, o_ssd_z, o_ssd_xbc)
    x1, h2t, q = _outproj(lru, ssd, w_out.astype(BF16), x2d, mod3, norm2_g, peer_w_q.astype(BF16), seq)
    s1, s2, tau = _route(q, (peer_sub_keys * math.log2(math.e)).astype(BF16))
    peer_t = _peer(h2t, peer_u, _transpose_bf16(peer_v), s1, s2, tau)
    return x1, peer_t


def kernel(x, c, w_ada, b_ada, norm1_g, w_in, lru_conv_w, lru_conv_b, lru_w_a, lru_b_a, lru_w_i, lru_b_i, lru_lambda, ssd_conv_w, ssd_conv_b, ssd_dt_bias, ssd_a_log, ssd_d, ssd_norm_g, w_out, norm2_g, peer_w_q, peer_sub_keys, peer_u, peer_v, final_norm_g):
    batch, seq, d = x.shape
    depth = w_ada.shape[0]
    assert depth == 1, "the final residual is fused with the last layer; only depth 1 is wired up"
    x2d = x.reshape(batch * seq, d)
    l = 0
    mod3 = _ada(c, w_ada[l], b_ada[l]).reshape(batch, 6, d)
    x1, peer_t = _layer(x2d, mod3, batch, seq, norm1_g[l], w_in[l], lru_conv_w[l], lru_conv_b[l], lru_w_a[l],
                        lru_b_a[l], lru_w_i[l], lru_b_i[l], lru_lambda[l], ssd_conv_w[l], ssd_conv_b[l],
                        ssd_dt_bias[l], ssd_a_log[l], ssd_d[l], ssd_norm_g[l], w_out[l], norm2_g[l],
                        peer_w_q[l], peer_sub_keys[l], peer_u[l], peer_v[l])
    y = _final(x1, peer_t, mod3, final_norm_g, seq)
    return y.reshape(batch, seq, d)
```

```python
import functools
import math

import jax
import jax.numpy as jnp
from jax import lax
from jax.experimental import pallas as pl
from jax.experimental.pallas import tpu as pltpu

F32 = jnp.float32
BF16 = jnp.bfloat16

EPS = 1e-6
CONV_WIDTH = 4
LRU_C = 8.0
LRU_HEAD_DIM = 128
SSD_HEAD_DIM = 64
SSD_GROUPS = 4
SSD_STATE = 128
SSD_CHUNK = 128
PEER_HEADS = 8
PEER_KEYS = 128
PEER_TOPK = 16
LANES = 128
SUBLANES = 8
NEG_BIG = -1e30
VMEM_LIMIT = 56 * 1024 * 1024


def _cparams(sem, flags=None):
    return pltpu.CompilerParams(dimension_semantics=sem, vmem_limit_bytes=VMEM_LIMIT, flags=flags)


def _split3(x):
    hi = x.astype(BF16)
    r1 = x - hi.astype(F32)
    mid = r1.astype(BF16)
    lo = (r1 - mid.astype(F32)).astype(BF16)
    return hi, mid, lo


def _dot_sel_right(x, sel):
    return sum(jnp.dot(p, sel, preferred_element_type=F32) for p in _split3(x))


def _dot_sel_left(sel, x):
    return sum(jnp.dot(sel, p, preferred_element_type=F32) for p in _split3(x))


def _softplus(x):
    return jnp.maximum(x, 0.0) + jnp.log1p(jnp.exp(-jnp.abs(x)))


def _silu(x):
    return x * jax.nn.sigmoid(x)


def _causal_conv(xpad, cw_ref, cb_ref, rows):
    e = xpad[0:SUBLANES + rows, :]
    z = cw_ref[0:1, :] * e
    for k in range(1, CONV_WIDTH):
        z = pltpu.roll(z, 1, 0) + cw_ref[k:k + 1, :] * e
    return z[SUBLANES:, :] + cb_ref[...]


def _gelu(x):
    c = math.sqrt(2.0 / math.pi)
    hx = 0.5 * x
    return hx + hx * jnp.tanh(x * (c + (c * 0.044715) * (x * x)))


def _ada_kernel(c_ref, w_ref, b_ref, o_ref, act_ref):
    nb = c_ref.shape[0]
    tn = w_ref.shape[1]

    @pl.when(pl.program_id(0) == 0)
    def _():
        cv = c_ref[...]
        act_ref[...] = _silu(cv)

    for k in range(tn // LANES):
        sl = slice(k * LANES, (k + 1) * LANES)
        wk = w_ref[:, sl]
        for b in range(nb):
            o_ref[b:b + 1, sl] = jnp.sum(wk * act_ref[b], axis=0, keepdims=True) + b_ref[:, sl]


def _ada(c, w_ada, b_ada):
    nb, d = c.shape
    n = w_ada.shape[1]
    tn = 1024
    cb = jnp.broadcast_to(c[:, :, None], (nb, d, LANES))
    return pl.pallas_call(
        _ada_kernel,
        grid=(n // tn,),
        in_specs=[
            pl.BlockSpec((nb, d, LANES), lambda j: (0, 0, 0)),
            pl.BlockSpec((d, tn), lambda j: (0, j)),
            pl.BlockSpec((1, tn), lambda j: (0, j)),
        ],
        out_specs=pl.BlockSpec((nb, tn), lambda j: (0, j)),
        out_shape=jax.ShapeDtypeStruct((nb, n), F32),
        scratch_shapes=[pltpu.VMEM((nb, d, LANES), F32)],
        compiler_params=_cparams(("arbitrary",)),
        name="ada",
    )(cb, w_ada, b_ada.reshape(1, n))


def _norm1_kernel(x_ref, mod_ref, g_ref, wdt_ref, h_ref, dt_ref):
    x = x_ref[...]
    y = x * lax.rsqrt(jnp.mean(x * x, axis=-1, keepdims=True) + EPS) * g_ref[...]
    h = (y * (1.0 + mod_ref[1:2, :]) + mod_ref[0:1, :]).astype(BF16)
    h_ref[...] = h
    dt_ref[...] = jnp.dot(h, wdt_ref[...], preferred_element_type=F32)


def _norm1(x2d, mod3, g, wdt, seq):
    t, d = x2d.shape
    tm = min(512, seq)
    per_batch = seq // tm
    return pl.pallas_call(
        _norm1_kernel,
        grid=(t // tm,),
        in_specs=[
            pl.BlockSpec((tm, d), lambda i: (i, 0)),
            pl.BlockSpec((None, 6, d), lambda i: (i // per_batch, 0, 0)),
            pl.BlockSpec((1, d), lambda i: (0, 0)),
            pl.BlockSpec((d, LANES), lambda i: (0, 0)),
        ],
        out_specs=[
            pl.BlockSpec((tm, d), lambda i: (i, 0)),
            pl.BlockSpec((tm, LANES), lambda i: (i, 0)),
        ],
        out_shape=[jax.ShapeDtypeStruct((t, d), BF16), jax.ShapeDtypeStruct((t, LANES), F32)],
        compiler_params=_cparams(("parallel",)),
        name="norm1",
    )(x2d, mod3, g.reshape(1, d), wdt)


def _inproj_kernel(h_ref, w_ref, o_ref):
    o_ref[...] = jnp.dot(h_ref[...], w_ref[...], preferred_element_type=F32)


def _inproj(h, w_bf, n_main):
    t, d = h.shape
    tn = 2304
    tm = min(1024, t)
    return pl.pallas_call(
        _inproj_kernel,
        grid=(n_main // tn, t // tm),
        in_specs=[
            pl.BlockSpec((tm, d), lambda j, i: (i, 0)),
            pl.BlockSpec((d, tn), lambda j, i: (0, j)),
        ],
        out_specs=pl.BlockSpec((tm, tn), lambda j, i: (i, j)),
        out_shape=jax.ShapeDtypeStruct((t, n_main), F32),
        compiler_params=_cparams(("parallel", "arbitrary")),
        name="inproj",
    )(h, w_bf)


def _lru_kernel(g_ref, x_ref, cw_ref, cb_ref, wab_ref, ba_ref, bi_ref, lam_ref, o_ref,
                xpad, a_scr, b_scr, hs, hcar):
    tt, c = x_ref.shape
    t = pl.program_id(2)

    @pl.when(t == 0)
    def _():
        xpad[0:SUBLANES, :] = jnp.zeros((SUBLANES, c), F32)
        hcar[...] = jnp.zeros_like(hcar)

    xpad[SUBLANES:SUBLANES + tt, :] = x_ref[...]
    y = _causal_conv(xpad, cw_ref, cb_ref, tt)
    xpad[0:SUBLANES, :] = xpad[tt:tt + SUBLANES, :]

    row = lax.broadcasted_iota(jnp.int32, (tt // SUBLANES, SUBLANES, LRU_HEAD_DIM), 1)
    for j in range(c // LRU_HEAD_DIM):
        sl = slice(j * LRU_HEAD_DIM, (j + 1) * LRU_HEAD_DIM)
        yj = y[:, sl]
        gm = jnp.dot(yj.astype(BF16), wab_ref[j], preferred_element_type=F32)
        r = jax.nn.sigmoid(gm[:, :LRU_HEAD_DIM] + ba_ref[:, sl])
        i = jax.nn.sigmoid(gm[:, LRU_HEAD_DIM:] + bi_ref[:, sl])
        log_a = (-LRU_C * _softplus(-lam_ref[:, sl])) * r
        a = jnp.exp(log_a)
        th = jnp.tanh(log_a)
        b = jnp.sqrt((-2.0 * th) / (1.0 - th)) * (i * yj)
        a = a.reshape(tt // SUBLANES, SUBLANES, LRU_HEAD_DIM)
        b = b.reshape(tt // SUBLANES, SUBLANES, LRU_HEAD_DIM)
        for s in (1, 2, 4):
            a_sh = pltpu.roll(a, s, 1)
            b_sh = pltpu.roll(b, s, 1)
            m = row >= s
            b = jnp.where(m, a * b_sh + b, b)
            a = jnp.where(m, a * a_sh, a)
        a_scr[:, sl] = a.reshape(tt, LRU_HEAD_DIM)
        b_scr[:, sl] = b.reshape(tt, LRU_HEAD_DIM)

    carry = hcar[...]
    for n in range(tt // SUBLANES):
        rs = slice(n * SUBLANES, (n + 1) * SUBLANES)
        hb = b_scr[rs, :] + a_scr[rs, :] * carry
        hs[rs, :] = hb
        carry = hb[SUBLANES - 1:SUBLANES, :]
    hcar[...] = carry
    o_ref[...] = (_gelu(g_ref[...]) * hs[...]).astype(BF16)


def _lru(proj, cw, cb, wab, ba, bi, lam, batch, seq, width, x_col0):
    t = proj.shape[0]
    c = 512
    tt = min(512, seq)
    nt = seq // tt
    ncb = width // c
    hpb = c // LRU_HEAD_DIM
    xoff = x_col0 // c
    return pl.pallas_call(
        _lru_kernel,
        grid=(batch, ncb, nt),
        in_specs=[
            pl.BlockSpec((tt, c), lambda b, j, i: (b * nt + i, j)),
            pl.BlockSpec((tt, c), lambda b, j, i: (b * nt + i, xoff + j)),
            pl.BlockSpec((CONV_WIDTH, c), lambda b, j, i: (0, j)),
            pl.BlockSpec((1, c), lambda b, j, i: (0, j)),
            pl.BlockSpec((hpb, LRU_HEAD_DIM, 2 * LRU_HEAD_DIM), lambda b, j, i: (j, 0, 0)),
            pl.BlockSpec((1, c), lambda b, j, i: (0, j)),
            pl.BlockSpec((1, c), lambda b, j, i: (0, j)),
            pl.BlockSpec((1, c), lambda b, j, i: (0, j)),
        ],
        out_specs=pl.BlockSpec((tt, c), lambda b, j, i: (b * nt + i, j)),
        out_shape=jax.ShapeDtypeStruct((t, width), BF16),
        scratch_shapes=[
            pltpu.VMEM((tt + SUBLANES, c), F32),
            pltpu.VMEM((tt, c), F32),
            pltpu.VMEM((tt, c), F32),
            pltpu.VMEM((tt, c), F32),
            pltpu.VMEM((1, c), F32),
        ],
        compiler_params=_cparams(("parallel", "parallel", "arbitrary")),
        name="lru",
    )(proj, proj, cw, cb, wab, ba, bi, lam)


def _ssd_kernel(z_ref, xbc_ref, dt_ref, cw_ref, cb_ref, dtb_ref, alog_ref, dsk_ref, ng_ref, e_ref, tri_ref,
                o_ref, xpad, st):
    L = SSD_CHUNK
    N = SSD_STATE
    w = z_ref.shape[1]
    gw = w // SSD_GROUPS
    gn = SSD_GROUPS * N

    @pl.when(pl.program_id(1) == 0)
    def _():
        xpad[0:SUBLANES, :] = jnp.zeros((SUBLANES, xpad.shape[1]), F32)
        st[...] = jnp.zeros_like(st)

    xpad[SUBLANES:SUBLANES + L, :] = xbc_ref[...]
    y = _causal_conv(xpad, cw_ref, cb_ref, L)
    xpad[0:SUBLANES, :] = xpad[L:L + SUBLANES, :]
    act = _silu(y)
    xs = act[:, :w]
    bm = act[:, w:w + gn]
    cm = act[:, w + gn:]

    dt = _softplus(dt_ref[...] + dtb_ref[...])
    da = dt * (-jnp.exp(alog_ref[...]))
    a_cs = _dot_sel_left(tri_ref[...], da)
    a_cs_t = a_cs.T
    ea = jnp.exp(a_cs)
    dte = jnp.exp(a_cs[L - 1:L, :] - a_cs)
    sel = e_ref[...]
    dt_f = _dot_sel_right(dt, sel)
    ea_f = _dot_sel_right(ea, sel)
    dte_f = _dot_sel_right(dte, sel)
    xc = xs * dt_f
    xcb = xc.astype(BF16)
    xd = (xc * dte_f).astype(BF16)

    causal = lax.broadcasted_iota(jnp.int32, (L, L), 0) >= lax.broadcasted_iota(jnp.int32, (L, L), 1)
    lane = lax.broadcasted_iota(jnp.int32, (L, LANES), 1)
    low_half = lane < SSD_HEAD_DIM
    heads_per_group = gw // SSD_HEAD_DIM
    for g in range(SSD_GROUPS):
        gs = slice(g * gw, (g + 1) * gw)
        cg = cm[:, g * N:(g + 1) * N].astype(BF16)
        bg = bm[:, g * N:(g + 1) * N]
        cbm = lax.dot_general(cg, bg.astype(BF16), (((1,), (1,)), ((), ())), preferred_element_type=F32)
        bg_t = bg.T.astype(BF16)
        s_prev = st[g]
        y_off = jnp.dot(cg, s_prev.astype(BF16), preferred_element_type=F32) * ea_f[:, gs]
        s_new = jnp.dot(bg_t, xd[:, gs], preferred_element_type=F32)
        st[g] = ea_f[L - 1:L, gs] * s_prev + s_new
        for p in range(heads_per_group // 2):
            h0 = g * heads_per_group + 2 * p
            cs = slice(h0 * SSD_HEAD_DIM, (h0 + 2) * SSD_HEAD_DIM)
            ms = []
            for hh in (h0, h0 + 1):
                seg = a_cs[:, hh:hh + 1] - a_cs_t[hh:hh + 1, :]
                ms.append(cbm * jnp.exp(jnp.where(causal, seg, NEG_BIG)))
            lhs = jnp.concatenate(ms, axis=1).astype(BF16)
            pair = xcb[:, cs]
            zero = jnp.zeros_like(pair)
            rhs = jnp.concatenate([jnp.where(low_half, pair, zero), jnp.where(low_half, zero, pair)], axis=0)
            y_diag = jnp.dot(lhs, rhs, preferred_element_type=F32)
            yp = y_diag + y_off[:, 2 * p * SSD_HEAD_DIM:(2 * p + 2) * SSD_HEAD_DIM] + dsk_ref[:, cs] * xs[:, cs]
            xpad[SUBLANES:SUBLANES + L, cs] = yp

    yz = xpad[SUBLANES:SUBLANES + L, 0:w] * _silu(z_ref[...])
    for g in range(SSD_GROUPS):
        gs = slice(g * gw, (g + 1) * gw)
        v = yz[:, gs]
        v = v * lax.rsqrt(jnp.mean(v * v, axis=-1, keepdims=True) + EPS)
        o_ref[:, gs] = (v * ng_ref[:, gs]).astype(BF16)


def _ssd(proj, dt_raw, cw, cb, dtb, alog, dsk, ng, sel, tri, batch, seq, width, z_col0, xbc_col0):
    t = proj.shape[0]
    xw = cw.shape[1]
    L = SSD_CHUNK
    nc = seq // L
    zoff = z_col0 // width
    xoff = xbc_col0 // xw
    return pl.pallas_call(
        _ssd_kernel,
        grid=(batch, nc),
        in_specs=[
            pl.BlockSpec((L, width), lambda b, i: (b * nc + i, zoff)),
            pl.BlockSpec((L, xw), lambda b, i: (b * nc + i, xoff)),
            pl.BlockSpec((L, LANES), lambda b, i: (b * nc + i, 0)),
            pl.BlockSpec((CONV_WIDTH, xw), lambda b, i: (0, 0)),
            pl.BlockSpec((1, xw), lambda b, i: (0, 0)),
            pl.BlockSpec((1, LANES), lambda b, i: (0, 0)),
            pl.BlockSpec((1, LANES), lambda b, i: (0, 0)),
            pl.BlockSpec((1, width), lambda b, i: (0, 0)),
            pl.BlockSpec((1, width), lambda b, i: (0, 0)),
            pl.BlockSpec((LANES, width), lambda b, i: (0, 0)),
            pl.BlockSpec((L, L), lambda b, i: (0, 0)),
        ],
        out_specs=pl.BlockSpec((L, width), lambda b, i: (b * nc + i, 0)),
        out_shape=jax.ShapeDtypeStruct((t, width), BF16),
        scratch_shapes=[
            pltpu.VMEM((L + SUBLANES, xw), F32),
            pltpu.VMEM((SSD_GROUPS, SSD_STATE, width // SSD_GROUPS), F32),
        ],
        compiler_params=_cparams(("parallel", "arbitrary")),
        name="ssd",
    )(proj, proj, dt_raw, cw, cb, dtb, alog, dsk, ng, sel, tri)


def _outproj_kernel(lru_ref, ssd_ref, wo_ref, x_ref, mod_ref, g_ref, wq_ref, x1_ref, h2t_ref, q_ref, acc):
    k = pl.program_id(1)
    nk = pl.num_programs(1)

    @pl.when(k == 0)
    def _():
        acc[...] = jnp.zeros_like(acc)

    @pl.when(k < nk // 2)
    def _():
        acc[...] += jnp.dot(lru_ref[...], wo_ref[...], preferred_element_type=F32)

    @pl.when(k >= nk // 2)
    def _():
        acc[...] += jnp.dot(ssd_ref[...], wo_ref[...], preferred_element_type=F32)

    @pl.when(k == nk - 1)
    def _():
        x1 = x_ref[...] + mod_ref[2:3, :] * acc[...]
        x1_ref[...] = x1
        y = x1 * lax.rsqrt(jnp.mean(x1 * x1, axis=-1, keepdims=True) + EPS) * g_ref[...]
        h2 = y * (1.0 + mod_ref[4:5, :]) + mod_ref[3:4, :]
        h2t_ref[...] = h2.T.astype(BF16)
        q_ref[...] = jnp.dot(h2.astype(BF16), wq_ref[...], preferred_element_type=F32)


def _outproj(lru, ssd, wo, x2d, mod3, g, wq, seq):
    t, d = x2d.shape
    half = lru.shape[1]
    tm = min(512, seq)
    tk = 1024
    nkh = half // tk
    per_batch = seq // tm
    nq = wq.shape[1]
    return pl.pallas_call(
        _outproj_kernel,
        grid=(t // tm, 2 * nkh),
        in_specs=[
            pl.BlockSpec((tm, tk), lambda i, k: (i, jnp.minimum(k, nkh - 1))),
            pl.BlockSpec((tm, tk), lambda i, k: (i, jnp.maximum(k - nkh, 0))),
            pl.BlockSpec((tk, d), lambda i, k: (k, 0)),
            pl.BlockSpec((tm, d), lambda i, k: (i, 0)),
            pl.BlockSpec((None, 6, d), lambda i, k: (i // per_batch, 0, 0)),
            pl.BlockSpec((1, d), lambda i, k: (0, 0)),
            pl.BlockSpec((d, nq), lambda i, k: (0, 0)),
        ],
        out_specs=[
            pl.BlockSpec((tm, d), lambda i, k: (i, 0)),
            pl.BlockSpec((d, tm), lambda i, k: (0, i)),
            pl.BlockSpec((tm, nq), lambda i, k: (i, 0)),
        ],
        out_shape=[
            jax.ShapeDtypeStruct((t, d), F32),
            jax.ShapeDtypeStruct((d, t), BF16),
            jax.ShapeDtypeStruct((t, nq), F32),
        ],
        scratch_shapes=[pltpu.VMEM((tm, d), F32)],
        compiler_params=_cparams(("parallel", "arbitrary")),
        name="outproj",
    )(lru, ssd, wo, x2d, mod3, g.reshape(1, d), wq)


def _sorting_network(n):
    pairs = []

    def merge(lo, hi, r):
        step = r * 2
        if step < hi - lo:
            merge(lo, hi, step)
            merge(lo + r, hi, step)
            pairs.extend((i, i + r) for i in range(lo + r, hi - r, step))
        else:
            pairs.append((lo, lo + r))

    def sort(lo, hi):
        if hi - lo >= 1:
            mid = lo + (hi - lo) // 2
            sort(lo, mid)
            sort(mid + 1, hi)
            merge(lo, hi, 1)

    sort(0, n - 1)
    return pairs


def _top_values(s, count):
    rows, tb = s.shape
    n = rows // SUBLANES
    assert n == count
    a = [s[k * SUBLANES:(k + 1) * SUBLANES, :] for k in range(n)]
    for i, j in _sorting_network(n):
        a[i], a[j] = jnp.maximum(a[i], a[j]), jnp.minimum(a[i], a[j])
    rid = lax.broadcasted_iota(jnp.int32, (count, tb), 0)
    out = jnp.zeros((count, tb), F32)
    for r in range(count):
        m = jnp.max(a[0], axis=0, keepdims=True)
        out = jnp.where(rid == r, m, out)
        hit = a[0] == m
        for k in range(count - 1 - r):
            a[k] = jnp.where(hit, a[k + 1], a[k])
    return out


def _pair_candidates(v1, v2):
    parts = [v1[0:1] + v2]
    parts += [v1[a:a + 1] + v2[0:8] for a in (1, 2, 3)]
    parts += [v1[0:8] + v2[b:b + 1] for b in (0, 1, 2)]
    parts.append(v1[8:16] + v2[0:1])
    return jnp.concatenate(parts, axis=0)


def _kth_and_expsum(cand, count):
    z = jnp.zeros((1, cand.shape[1]), F32)
    m = None
    for _ in range(count):
        m = jnp.max(cand, axis=0, keepdims=True)
        z = z + jnp.exp2(m)
        cand = jnp.where(cand == m, NEG_BIG, cand)
    return m, z


def _route_kernel(q_ref, keys_ref, s1_ref, s2_ref, tau_ref):
    for h in range(PEER_HEADS):
        ss = []
        for half in range(2):
            col = (2 * h + half) * LANES
            qh = q_ref[:, col:col + LANES].astype(BF16)
            s = lax.dot_general(keys_ref[h, half], qh, (((1,), (1,)), ((), ())), preferred_element_type=F32)
            ss.append(s - jnp.max(s, axis=0, keepdims=True))
        s1, s2 = ss
        v1 = _top_values(s1, PEER_TOPK)
        v2 = _top_values(s2, PEER_TOPK)
        _, z = _kth_and_expsum(_pair_candidates(v1, v2), PEER_TOPK)
        lz = jnp.log2(z)
        s1 = s1 - lz
        tau, _ = _kth_and_expsum(_pair_candidates(v1 - lz, v2), PEER_TOPK)
        s1_ref[h] = s1
        s2_ref[h] = s2
        tau_ref[h:h + 1, :] = tau


def _route(q, keys):
    t, nq = q.shape
    tb = min(512, t)
    nk = keys.shape[2]
    return pl.pallas_call(
        _route_kernel,
        grid=(t // tb,),
        in_specs=[
            pl.BlockSpec((tb, nq), lambda i: (i, 0)),
            pl.BlockSpec(keys.shape, lambda i: (0, 0, 0, 0)),
        ],
        out_specs=[
            pl.BlockSpec((PEER_HEADS, nk, tb), lambda i: (0, 0, i)),
            pl.BlockSpec((PEER_HEADS, nk, tb), lambda i: (0, 0, i)),
            pl.BlockSpec((PEER_HEADS, tb), lambda i: (0, i)),
        ],
        out_shape=[
            jax.ShapeDtypeStruct((PEER_HEADS, nk, t), F32),
            jax.ShapeDtypeStruct((PEER_HEADS, nk, t), F32),
            jax.ShapeDtypeStruct((PEER_HEADS, t), F32),
        ],
        compiler_params=_cparams(("parallel",)),
        name="route",
    )(q, keys)


def _vt_kernel(v_ref, o_ref):
    o_ref[...] = v_ref[...].T.astype(BF16)


def _transpose_bf16(v):
    ne, d = v.shape
    eb = 512
    return pl.pallas_call(
        _vt_kernel,
        grid=(ne // eb,),
        in_specs=[pl.BlockSpec((eb, d), lambda e: (e, 0))],
        out_specs=pl.BlockSpec((d, eb), lambda e: (0, e)),
        out_shape=jax.ShapeDtypeStruct((d, ne), BF16),
        compiler_params=_cparams(("parallel",)),
        name="vt",
    )(v)


def _peer_kernel(h2t_ref, u_ref, vt_ref, s1_ref, s2_ref, tau_ref, o_ref, at_scr, w_scr, *, tc):
    e = pl.program_id(1)
    ts = h2t_ref.shape[1]
    eb = u_ref.shape[0]

    @pl.when(e == 0)
    def _():
        o_ref[...] = jnp.zeros_like(o_ref)

    for c in range(ts // tc):
        cs = slice(c * tc, (c + 1) * tc)
        tiles = eb // PEER_KEYS
        row0 = (e % (SUBLANES // tiles)) * tiles
        for it in range(tiles):
            rows = slice(it * PEER_KEYS, (it + 1) * PEER_KEYS)
            s1_rows = [s1_ref[h, pl.ds(row0 + it, 1), :] for h in range(PEER_HEADS)]
            for lc in range(tc // LANES):
                ls = slice(c * tc + lc * LANES, c * tc + (lc + 1) * LANES)
                wsum = None
                for h in range(PEER_HEADS):
                    tsum = s1_rows[h][:, ls] + s2_ref[h, :, ls]
                    term = jnp.where(tsum >= tau_ref[h:h + 1, ls], jnp.exp2(tsum), 0.0)
                    wsum = term if wsum is None else wsum + term
                w_scr[rows, lc * LANES:(lc + 1) * LANES] = wsum
        at_scr[...] = jnp.dot(u_ref[...].astype(BF16), h2t_ref[:, cs], preferred_element_type=F32)
        g = (_gelu(at_scr[...]) * w_scr[...]).astype(BF16)
        o_ref[:, cs] += jnp.dot(vt_ref[...], g, preferred_element_type=F32)


def _peer(h2t, u, vt, s1, s2, tau):
    d, t = h2t.shape
    ne = u.shape[0]
    ts = min(1024, t)
    tc = ts
    eb = 512
    rows_per_blk = eb // PEER_KEYS
    return pl.pallas_call(
        functools.partial(_peer_kernel, tc=tc),
        grid=(t // ts, ne // eb),
        in_specs=[
            pl.BlockSpec((d, ts), lambda s, e: (0, s)),
            pl.BlockSpec((eb, d), lambda s, e: (e, 0)),
            pl.BlockSpec((d, eb), lambda s, e: (0, e)),
            pl.BlockSpec((PEER_HEADS, SUBLANES, ts), lambda s, e: (0, e * rows_per_blk // SUBLANES, s)),
            pl.BlockSpec((PEER_HEADS, PEER_KEYS, ts), lambda s, e: (0, 0, s)),
            pl.BlockSpec((PEER_HEADS, ts), lambda s, e: (0, s)),
        ],
        out_specs=pl.BlockSpec((d, ts), lambda s, e: (0, s)),
        out_shape=jax.ShapeDtypeStruct((d, t), F32),
        scratch_shapes=[pltpu.VMEM((eb, tc), F32), pltpu.VMEM((eb, tc), F32)],
        compiler_params=_cparams(("parallel", "arbitrary")),
        name="peer",
    )(h2t, u, vt, s1, s2, tau)


def _final_kernel(x1_ref, pt_ref, mod_ref, g_ref, o_ref):
    x2 = x1_ref[...] + mod_ref[5:6, :] * pt_ref[...].T
    o_ref[...] = x2 * lax.rsqrt(jnp.mean(x2 * x2, axis=-1, keepdims=True) + EPS) * g_ref[...]


def _final(x1, peer_t, mod3, g, seq):
    t, d = x1.shape
    tm = min(512, seq)
    per_batch = seq // tm
    return pl.pallas_call(
        _final_kernel,
        grid=(t // tm,),
        in_specs=[
            pl.BlockSpec((tm, d), lambda i: (i, 0)),
            pl.BlockSpec((d, tm), lambda i: (0, i)),
            pl.BlockSpec((None, 6, d), lambda i: (i // per_batch, 0, 0)),
            pl.BlockSpec((1, d), lambda i: (0, 0)),
        ],
        out_specs=pl.BlockSpec((tm, d), lambda i: (i, 0)),
        out_shape=jax.ShapeDtypeStruct((t, d), F32),
        compiler_params=_cparams(("parallel",)),
        name="final",
    )(x1, peer_t, mod3, g.reshape(1, d))


def _layer(x2d, mod3, batch, seq, norm1_g, w_in, lru_conv_w, lru_conv_b, lru_w_a, lru_b_a, lru_w_i, lru_b_i,
           lru_lambda, ssd_conv_w, ssd_conv_b, ssd_dt_bias, ssd_a_log, ssd_d, ssd_norm_g, w_out, norm2_g,
           peer_w_q, peer_sub_keys, peer_u, peer_v):
    d = x2d.shape[1]
    lru_w = lru_lambda.shape[0]
    ssd_w = ssd_norm_g.shape[0]
    xbc_w = ssd_conv_w.shape[1]
    n_heads = ssd_dt_bias.shape[0]
    o_lru_x = lru_w
    o_ssd_z = 2 * lru_w
    o_ssd_xbc = o_ssd_z + ssd_w
    o_ssd_dt = o_ssd_xbc + xbc_w

    w_bf = w_in.astype(BF16)
    wdt = jnp.pad(w_bf[:, o_ssd_dt:o_ssd_dt + n_heads], ((0, 0), (0, LANES - n_heads)))
    wab = jnp.concatenate([lru_w_a, lru_w_i], axis=-1).astype(BF16)
    pad_h = lambda v: jnp.pad(v, (0, LANES - n_heads)).reshape(1, LANES)
    head_of_channel = jnp.arange(ssd_w, dtype=jnp.int32) // SSD_HEAD_DIM
    sel = (jnp.arange(LANES, dtype=jnp.int32)[:, None] == head_of_channel[None, :]).astype(BF16)
    tri = (jnp.arange(SSD_CHUNK)[:, None] >= jnp.arange(SSD_CHUNK)[None, :]).astype(BF16)
    dsk = jnp.repeat(ssd_d, SSD_HEAD_DIM).reshape(1, ssd_w)

    h, dt_raw = _norm1(x2d, mod3, norm1_g, wdt, seq)
    proj = _inproj(h, w_bf, o_ssd_dt)
    lru = _lru(proj, lru_conv_w, lru_conv_b.reshape(1, lru_w), wab, lru_b_a.reshape(1, lru_w),
               lru_b_i.reshape(1, lru_w), lru_lambda.reshape(1, lru_w), batch, seq, lru_w, o_lru_x)
    ssd = _ssd(proj, dt_raw, ssd_conv_w, ssd_conv_b.reshape(1, xbc_w), pad_h(ssd_dt_bias), pad_h(ssd_a_log),
               dsk, ssd_norm_g.reshape(1, ssd_w), sel, tri, batch, seq, ssd_w, o_ssd_z, o_ssd_xbc)
    x1, h2t, q = _outproj(lru, ssd, w_out.astype(BF16), x2d, mod3, norm2_g, peer_w_q.astype(BF16), seq)
    s1, s2, tau = _route(q, (peer_sub_keys * math.log2(math.e)).astype(BF16))
    peer_t = _peer(h2t, peer_u, _transpose_bf16(peer_v), s1, s2, tau)
    return x1, peer_t


def kernel(x, c, w_ada, b_ada, norm1_g, w_in, lru_conv_w, lru_conv_b, lru_w_a, lru_b_a, lru_w_i, lru_b_i, lru_lambda, ssd_conv_w, ssd_conv_b, ssd_dt_bias, ssd_a_log, ssd_d, ssd_norm_g, w_out, norm2_g, peer_w_q, peer_sub_keys, peer_u, peer_v, final_norm_g):
    batch, seq, d = x.shape
    depth = w_ada.shape[0]
    assert depth == 1, "the final residual is fused with the last layer; only depth 1 is wired up"
    x2d = x.reshape(batch * seq, d)
    l = 0
    mod3 = _ada(c, w_ada[l], b_ada[l]).reshape(batch, 6, d)
    x1, peer_t = _layer(x2d, mod3, batch, seq, norm1_g[l], w_in[l], lru_conv_w[l], lru_conv_b[l], lru_w_a[l],
                        lru_b_a[l], lru_w_i[l], lru_b_i[l], lru_lambda[l], ssd_conv_w[l], ssd_conv_b[l],
                        ssd_dt_bias[l], ssd_a_log[l], ssd_d[l], ssd_norm_g[l], w_out[l], norm2_g[l],
                        peer_w_q[l], peer_sub_keys[l], peer_u[l], peer_v[l])
    y = _final(x1, peer_t, mod3, final_norm_g, seq)
    return y.reshape(batch, seq, d)
```
